```python
import jax, jax.numpy as jnp
from jax import lax
import numpy as np

D_MODEL = 2048
BATCH = 4
SEQ = 4096
DEPTH = 2

N_MIXERS = 2
N_HEADS = 16
HEAD_DIM = D_MODEL // N_HEADS
Q_BLOCK = 128
CONV_WIDTH = 3
D_FF = 4 * D_MODEL
N_MOD = 6
RMS_EPS = 1e-6
FORGET_BIAS_CENTER = 3.0

kernel_name = 'fox_shortconv_hybrid_adaln'


def rmsnorm(x, gain):
    xf = x.astype(jnp.float32)
    inv = lax.rsqrt(jnp.mean(xf * xf, axis=-1, keepdims=True) + RMS_EPS)
    return (xf * inv).astype(x.dtype) * gain


def modulate(h, shift, scale):
    return h * (1.0 + scale[:, None, :]) + shift[:, None, :]


def fox_attention(q, k, v, log_f):
    B, H, S, Dh = q.shape
    nb = S // Q_BLOCK
    F = jnp.cumsum(log_f.astype(jnp.float32), axis=-1)
    q_blocks = q.reshape(B, H, nb, Q_BLOCK, Dh).transpose(2, 0, 1, 3, 4)
    F_blocks = F.reshape(B, H, nb, Q_BLOCK).transpose(2, 0, 1, 3)
    k_pos = jnp.arange(S)
    scale = HEAD_DIM ** -0.5

    def one_block(args):
        blk, q_i, F_i = args
        s = jnp.einsum('bhqd,bhkd->bhqk', q_i, k).astype(jnp.float32) * scale
        s = s + F_i[..., :, None] - F[..., None, :]
        q_pos = blk * Q_BLOCK + jnp.arange(Q_BLOCK)
        causal = k_pos[None, :] <= q_pos[:, None]
        s = jnp.where(causal, s, -jnp.inf)
        p = jax.nn.softmax(s, axis=-1)
        return jnp.einsum('bhqk,bhkd->bhqd', p.astype(v.dtype), v)

    out = lax.map(one_block, (jnp.arange(nb), q_blocks, F_blocks))
    return out.transpose(1, 2, 0, 3, 4).reshape(B, H, S, Dh)


def fox_mixer(h, w_in, b_f, w_out):
    B, S, D = h.shape
    proj = h @ w_in
    q, k, v, f_logit = jnp.split(proj, [D, 2 * D, 3 * D], axis=-1)
    to_heads = lambda t: t.reshape(B, S, N_HEADS, HEAD_DIM).transpose(0, 2, 1, 3)
    log_f = jax.nn.log_sigmoid((f_logit + b_f).astype(jnp.float32)).transpose(0, 2, 1)
    o = fox_attention(to_heads(q), to_heads(k), to_heads(v), log_f)
    o = o.transpose(0, 2, 1, 3).reshape(B, S, D)
    return o @ w_out


def short_conv_mixer(h, w_in, conv_w, w_out):
    S = h.shape[1]
    proj = h @ w_in
    b_gate, c_gate, u = jnp.split(proj, 3, axis=-1)
    u = c_gate * u
    u_pad = jnp.pad(u, ((0, 0), (CONV_WIDTH - 1, 0), (0, 0)))
    y = sum(conv_w[tap] * u_pad[:, tap:tap + S, :] for tap in range(CONV_WIDTH))
    return (b_gate * y) @ w_out


def sq_relu_mlp(h, w_up, w_down):
    return jnp.square(jax.nn.relu(h @ w_up)) @ w_down


def setup_inputs(seed: int = 0) -> dict:
    key = jax.random.key(seed)
    ks = jax.random.split(key, 16)
    D = D_MODEL
    n_fox = (DEPTH + 1) // 2
    n_conv = DEPTH // 2
    nrm = lambda k, shape, fan_in, mult=1.0: jax.random.normal(k, shape, jnp.float32) * (mult * fan_in ** -0.5)
    return {
        'x': jax.random.normal(ks[0], (BATCH, SEQ, D), jnp.float32),
        'c': jax.random.normal(ks[1], (BATCH, D), jnp.float32),
        'ada_w': nrm(ks[2], (DEPTH, D, N_MOD * D), D, 0.5),
        'ada_b': 0.01 * jax.random.normal(ks[3], (DEPTH, N_MOD * D), jnp.float32),
        'norm_mix': 1.0 + 0.02 * jax.random.normal(ks[4], (DEPTH, D), jnp.float32),
        'norm_mlp': 1.0 + 0.02 * jax.random.normal(ks[5], (DEPTH, D), jnp.float32),
        'fox_w_in': nrm(ks[6], (n_fox, D, 3 * D + N_HEADS), D),
        'fox_b_f': FORGET_BIAS_CENTER + 0.1 * jax.random.normal(ks[7], (n_fox, N_HEADS), jnp.float32),
        'fox_w_out': nrm(ks[8], (n_fox, D, D), D),
        'conv_w_in': nrm(ks[9], (n_conv, D, 3 * D), D),
        'conv_w': nrm(ks[10], (n_conv, CONV_WIDTH, D), CONV_WIDTH),
        'conv_w_out': nrm(ks[11], (n_conv, D, D), D),
        'mlp_w_up': nrm(ks[12], (DEPTH, D, D_FF), D),
        'mlp_w_down': nrm(ks[13], (DEPTH, D_FF, D), D_FF),
        'final_norm': 1.0 + 0.02 * jax.random.normal(ks[14], (D,), jnp.float32),
    }


def reference(x, c, ada_w, ada_b, norm_mix, norm_mlp, fox_w_in, fox_b_f, fox_w_out,
              conv_w_in, conv_w, conv_w_out, mlp_w_up, mlp_w_down, final_norm):
    c_act = jax.nn.silu(c)
    for i in range(DEPTH):
        mod = c_act @ ada_w[i] + ada_b[i]
        sh_mix, sc_mix, g_mix, sh_mlp, sc_mlp, g_mlp = jnp.split(mod, N_MOD, axis=-1)
        h = modulate(rmsnorm(x, norm_mix[i]), sh_mix, sc_mix)
        j = i // N_MIXERS
        if i % N_MIXERS == 0:
            mix = fox_mixer(h, fox_w_in[j], fox_b_f[j], fox_w_out[j])
        else:
            mix = short_conv_mixer(h, conv_w_in[j], conv_w[j], conv_w_out[j])
        x = x + g_mix[:, None, :] * mix
        h = modulate(rmsnorm(x, norm_mlp[i]), sh_mlp, sc_mlp)
        x = x + g_mlp[:, None, :] * sq_relu_mlp(h, mlp_w_up[i], mlp_w_down[i])
    return rmsnorm(x, final_norm)
```

```python
import functools

import jax
import jax.numpy as jnp
from jax import lax
from jax.experimental import pallas as pl
from jax.experimental.pallas import tpu as pltpu

N_HEADS = 16
CONV_WIDTH = 3
N_MOD = 6
RMS_EPS = 1e-6

LANES = 128
BF16_SUBLANES = 16
V7X_VMEM_BYTES = 64 * 1024 * 1024
VMEM_LIMIT_BYTES = V7X_VMEM_BYTES - 8 * 1024 * 1024

F32 = jnp.float32
BF16 = jnp.bfloat16


def _params(*semantics):
    return pltpu.CompilerParams(dimension_semantics=semantics, vmem_limit_bytes=VMEM_LIMIT_BYTES)


def _dot(a, b):
    return jnp.dot(a, b, preferred_element_type=F32)


def _norm_modulate(x, gain, shift, scale):
    inv = lax.rsqrt(jnp.mean(x * x, axis=-1, keepdims=True) + RMS_EPS)
    return (x * inv) * gain * (1.0 + scale) + shift


def _ada_kernel(c_ref, w_ref, b_ref, o_ref):
    c = c_ref[...]
    act = (c * jax.nn.sigmoid(c)).astype(BF16)
    o_ref[...] = _dot(act, w_ref[...].astype(BF16)) + b_ref[...]


def _ada_modulation(c_pad, ada_w, ada_b, tn=1024):
    depth, d, n = ada_w.shape
    rows = c_pad.shape[0]
    return pl.pallas_call(
        _ada_kernel,
        grid=(depth, n // tn),
        in_specs=[
            pl.BlockSpec((rows, d), lambda l, j: (0, 0)),
            pl.BlockSpec((None, d, tn), lambda l, j: (l, 0, j)),
            pl.BlockSpec((None, 1, tn), lambda l, j: (l, 0, j)),
        ],
        out_specs=pl.BlockSpec((None, rows, tn), lambda l, j: (l, 0, j)),
        out_shape=jax.ShapeDtypeStruct((depth, rows, n), F32),
        compiler_params=_params("parallel", "parallel"),
        name="ada_modulation",
    )(c_pad, ada_w, ada_b.reshape(depth, 1, n))


def _mod_spec(layer, which, tiles_per_batch, d):
    return pl.BlockSpec((None, None, None, 1, d),
                        lambda i, *_: (layer, i // tiles_per_batch, which, 0, 0))


def _inproj_kernel(*refs, n_scaled_tiles, out_scale, with_forget):
    if with_forget:
        x_ref, g_ref, sh_ref, sc_ref, w_ref, wf_ref, o_ref, f_ref, h_scr = refs
    else:
        x_ref, g_ref, sh_ref, sc_ref, w_ref, o_ref, h_scr = refs
    j = pl.program_id(1)

    @pl.when(j == 0)
    def _():
        h = _norm_modulate(x_ref[...], g_ref[...], sh_ref[...], sc_ref[...]).astype(BF16)
        h_scr[...] = h
        if with_forget:
            f_ref[...] = _dot(h, wf_ref[...])

    acc = _dot(h_scr[...], w_ref[...])
    if n_scaled_tiles:
        acc = acc * jnp.where(j < n_scaled_tiles, out_scale, 1.0)
    o_ref[...] = acc.astype(o_ref.dtype)


def _inproj(x, gain, mod, layer, w, wf, *, seq, n_scaled_cols=0, out_scale=1.0, tm=1024, tn=1024):
    m, d = x.shape
    n = w.shape[1]
    tpb = seq // tm
    with_forget = wf is not None
    in_specs = [
        pl.BlockSpec((tm, d), lambda i, j: (i, 0)),
        pl.BlockSpec((1, d), lambda i, j: (0, 0)),
        _mod_spec(layer, 0, tpb, d),
        _mod_spec(layer, 1, tpb, d),
        pl.BlockSpec((d, tn), lambda i, j: (0, j)),
    ]
    args = [x, gain, mod, mod, w]
    out_specs = [pl.BlockSpec((tm, tn), lambda i, j: (i, j))]
    out_shape = [jax.ShapeDtypeStruct((m, n), BF16)]
    if with_forget:
        in_specs.append(pl.BlockSpec((d, LANES), lambda i, j: (0, 0)))
        args.append(wf)
        out_specs.append(pl.BlockSpec((tm, LANES), lambda i, j: (i, 0)))
        out_shape.append(jax.ShapeDtypeStruct((m, LANES), F32))
    kern = functools.partial(_inproj_kernel, n_scaled_tiles=n_scaled_cols // tn,
                             out_scale=out_scale, with_forget=with_forget)
    return pl.pallas_call(
        kern,
        grid=(m // tm, n // tn),
        in_specs=in_specs,
        out_specs=out_specs,
        out_shape=out_shape,
        scratch_shapes=[pltpu.VMEM((tm, d), BF16)],
        compiler_params=_params("parallel", "arbitrary"),
        name="inproj_forget" if with_forget else "inproj",
    )(*args)


def _split3(x):
    hi = x.astype(BF16).astype(F32)
    r = x - hi
    mid = r.astype(BF16).astype(F32)
    lo = (r - mid).astype(BF16).astype(F32)
    return hi, mid, lo


def _forget_cumsum_kernel(fl_ref, bf_ref, o_ref, carry):
    @pl.when(pl.program_id(1) == 0)
    def _():
        carry[...] = jnp.zeros_like(carry)

    z = fl_ref[...] + bf_ref[...]
    log_f = jnp.minimum(z, 0.0) - jnp.log1p(jnp.exp(-jnp.abs(z)))
    tc = z.shape[0]
    row = lax.broadcasted_iota(jnp.int32, (tc, tc), 0)
    col = lax.broadcasted_iota(jnp.int32, (tc, tc), 1)
    tri = jnp.where(col <= row, 1.0, 0.0).astype(BF16)
    hi, mid, lo = _split3(log_f)
    csum = (_dot(tri, hi.astype(BF16)) + _dot(tri, mid.astype(BF16))) + _dot(tri, lo.astype(BF16))
    out = csum + carry[...]
    o_ref[...] = out
    carry[...] = out[tc - 1:tc, :]


def _forget_cumsum(f_logit, b_f_pad, *, batch, seq, tc=512):
    fl = f_logit.reshape(batch, seq, LANES)
    return pl.pallas_call(
        _forget_cumsum_kernel,
        grid=(batch, seq // tc),
        in_specs=[
            pl.BlockSpec((None, tc, LANES), lambda b, i: (b, i, 0)),
            pl.BlockSpec((1, LANES), lambda b, i: (0, 0)),
        ],
        out_specs=pl.BlockSpec((None, tc, LANES), lambda b, i: (b, i, 0)),
        out_shape=jax.ShapeDtypeStruct((batch, seq, LANES), F32),
        scratch_shapes=[pltpu.VMEM((1, LANES), F32)],
        compiler_params=_params("parallel", "arbitrary"),
        name="forget_cumsum",
    )(fl, b_f_pad)


def _fox_attn_kernel(q_ref, k_ref, v_ref, f_ref, o_ref, kaug, *, tq):
    head = pl.program_id(1)
    qi = pl.program_id(2)
    seq, dh = k_ref.shape
    lane = lax.broadcasted_iota(jnp.int32, (1, LANES), 1)

    def head_column(rows):
        return jnp.sum(jnp.where(lane == head, rows, 0.0), axis=-1, keepdims=True)

    @pl.when(qi == 0)
    def _():
        for c in range(seq // tq):
            rows = pl.ds(c * tq, tq)
            hi, mid, lo = _split3(-head_column(f_ref[rows, :]))
            aux = jnp.where(lane < 3, 1.0,
                            jnp.where(lane == 3, hi, jnp.where(lane == 4, mid,
                                                               jnp.where(lane == 5, lo, 0.0))))
            kaug[rows, 0:dh] = k_ref[rows, :]
            kaug[rows, dh:dh + LANES] = aux.astype(BF16)

    q_rows = pl.ds(pl.multiple_of(qi * tq, tq), tq)
    hi, mid, lo = _split3(head_column(f_ref[q_rows, :]))
    qaux = jnp.where(lane == 0, hi,
                     jnp.where(lane == 1, mid,
                               jnp.where(lane == 2, lo, jnp.where(lane < 6, 1.0, 0.0))))
    qa = jnp.concatenate([q_ref[...], qaux.astype(BF16)], axis=-1)

    def step(j, carry, masked):
        m, l, acc = carry
        k_rows = pl.ds(pl.multiple_of(j * tq, tq), tq)
        s = lax.dot_general(qa, kaug[k_rows, :], (((1,), (1,)), ((), ())),
                            preferred_element_type=F32)
        if masked:
            row = lax.broadcasted_iota(jnp.int32, s.shape, 0)
            col = lax.broadcasted_iota(jnp.int32, s.shape, 1)
            s = jnp.where(col <= row, s, -jnp.inf)
        m_new = jnp.maximum(m, jnp.max(s, axis=-1, keepdims=True))
        alpha = jnp.exp(m - m_new)
        p = jnp.exp(s - m_new)
        l_new = alpha * l + jnp.sum(p, axis=-1, keepdims=True)
        acc_new = alpha * acc + _dot(p.astype(BF16), v_ref[k_rows, :])
        return m_new, l_new, acc_new

    init = (jnp.full((tq, 1), -jnp.inf, F32), jnp.zeros((tq, 1), F32), jnp.zeros((tq, dh), F32))
    carry = lax.fori_loop(0, qi, functools.partial(step, masked=False), init)
    _, l, acc = step(qi, carry, masked=True)
    o_ref[...] = (acc / l).astype(o_ref.dtype)


def _fox_attention(qkv, fcum, *, batch, seq, d, tq=512):
    dh = d // N_HEADS
    qkv3 = qkv.reshape(batch, seq, 3 * d)
    return pl.pallas_call(
        functools.partial(_fox_attn_kernel, tq=tq),
        grid=(batch, N_HEADS, seq // tq),
        in_specs=[
            pl.BlockSpec((None, tq, dh), lambda b, h, i: (b, i, h)),
            pl.BlockSpec((None, seq, dh), lambda b, h, i: (b, 0, N_HEADS + h)),
            pl.BlockSpec((None, seq, dh), lambda b, h, i: (b, 0, 2 * N_HEADS + h)),
            pl.BlockSpec((None, seq, LANES), lambda b, h, i: (b, 0, 0)),
        ],
        out_specs=pl.BlockSpec((None, tq, dh), lambda b, h, i: (b, i, h)),
        out_shape=jax.ShapeDtypeStruct((batch, seq, d), BF16),
        scratch_shapes=[pltpu.VMEM((seq, dh + LANES), BF16)],
        compiler_params=_params("parallel", "parallel", "arbitrary"),
        name="fox_attention",
    )(qkv3, qkv3, qkv3, fcum).reshape(batch * seq, d)


def _outproj_kernel(a_ref, w_ref, x_ref, g_ref, o_ref):
    o_ref[...] = x_ref[...] + g_ref[...] * _dot(a_ref[...], w_ref[...])


def _outproj(a, w, x, mod, layer, *, seq, tm=512):
    m, d = x.shape
    tpb = seq // tm
    return pl.pallas_call(
        _outproj_kernel,
        grid=(m // tm,),
        in_specs=[
            pl.BlockSpec((tm, d), lambda i: (i, 0)),
            pl.BlockSpec((d, d), lambda i: (0, 0)),
            pl.BlockSpec((tm, d), lambda i: (i, 0)),
            _mod_spec(layer, 2, tpb, d),
        ],
        out_specs=pl.BlockSpec((tm, d), lambda i: (i, 0)),
        out_shape=jax.ShapeDtypeStruct((m, d), F32),
        compiler_params=_params("parallel"),
        name="attn_outproj",
    )(a, w, x, mod)


def _conv_outproj_kernel(b_ref, c_ref, u_ref, ch_ref, uh_ref, cw_ref, w_ref, x_ref, g_ref, o_ref,
                         ext, *, tiles_per_seq):
    tm = c_ref.shape[0]
    halo_rows = ch_ref.shape[0]
    seq_start = pl.program_id(0) % tiles_per_seq == 0
    halo = ch_ref[...].astype(F32) * uh_ref[...].astype(F32)
    ext[0:halo_rows, :] = jnp.where(seq_start, 0.0, halo)
    ext[halo_rows:halo_rows + tm, :] = c_ref[...].astype(F32) * u_ref[...].astype(F32)
    y = None
    for tap in range(CONV_WIDTH):
        start = halo_rows - (CONV_WIDTH - 1 - tap)
        term = cw_ref[tap:tap + 1, :] * ext[start:start + tm, :]
        y = term if y is None else y + term
    z = (b_ref[...].astype(F32) * y).astype(BF16)
    o_ref[...] = x_ref[...] + g_ref[...] * _dot(z, w_ref[...])


def _conv_outproj(proj, conv_w, w, x, mod, layer, *, seq, tm=512):
    m, d = x.shape
    tpb = seq // tm
    halo = BF16_SUBLANES
    halo_idx = lambda col: (lambda i: (jnp.maximum(i * (tm // halo) - 1, 0), col))
    return pl.pallas_call(
        functools.partial(_conv_outproj_kernel, tiles_per_seq=tpb),
        grid=(m // tm,),
        in_specs=[
            pl.BlockSpec((tm, d), lambda i: (i, 0)),
            pl.BlockSpec((tm, d), lambda i: (i, 1)),
            pl.BlockSpec((tm, d), lambda i: (i, 2)),
            pl.BlockSpec((halo, d), halo_idx(1)),
            pl.BlockSpec((halo, d), halo_idx(2)),
            pl.BlockSpec((CONV_WIDTH, d), lambda i: (0, 0)),
            pl.BlockSpec((d, d), lambda i: (0, 0)),
            pl.BlockSpec((tm, d), lambda i: (i, 0)),
            _mod_spec(layer, 2, tpb, d),
        ],
        out_specs=pl.BlockSpec((tm, d), lambda i: (i, 0)),
        out_shape=jax.ShapeDtypeStruct((m, d), F32),
        scratch_shapes=[pltpu.VMEM((halo + tm, d), F32)],
        compiler_params=_params("parallel"),
        name="conv_outproj",
    )(proj, proj, proj, proj, proj, conv_w, w, x, mod)


def _mlp_kernel(x_ref, g_ref, sh_ref, sc_ref, gate_ref, wu_ref, wd_ref, fin_ref, o_ref,
                h_scr, acc, *, final_norm):
    f = pl.program_id(1)

    @pl.when(f == 0)
    def _():
        h_scr[...] = _norm_modulate(x_ref[...], g_ref[...], sh_ref[...], sc_ref[...]).astype(BF16)
        acc[...] = jnp.zeros_like(acc)

    u = jnp.maximum(_dot(h_scr[...], wu_ref[...]), 0.0)
    acc[...] += _dot((u * u).astype(BF16), wd_ref[...])

    @pl.when(f == pl.num_programs(1) - 1)
    def _():
        y = x_ref[...] + gate_ref[...] * acc[...]
        if final_norm:
            inv = lax.rsqrt(jnp.mean(y * y, axis=-1, keepdims=True) + RMS_EPS)
            y = (y * inv) * fin_ref[...]
        o_ref[...] = y


def _mlp(x, gain, mod, layer, w_up, w_down, fin_gain, *, seq, final_norm, tm=512, tf=1024):
    m, d = x.shape
    dff = w_up.shape[1]
    tpb = seq // tm
    return pl.pallas_call(
        functools.partial(_mlp_kernel, final_norm=final_norm),
        grid=(m // tm, dff // tf),
        in_specs=[
            pl.BlockSpec((tm, d), lambda i, f: (i, 0)),
            pl.BlockSpec((1, d), lambda i, f: (0, 0)),
            _mod_spec(layer, 3, tpb, d),
            _mod_spec(layer, 4, tpb, d),
            _mod_spec(layer, 5, tpb, d),
            pl.BlockSpec((d, tf), lambda i, f: (0, f)),
            pl.BlockSpec((tf, d), lambda i, f: (f, 0)),
            pl.BlockSpec((1, d), lambda i, f: (0, 0)),
        ],
        out_specs=pl.BlockSpec((tm, d), lambda i, f: (i, 0)),
        out_shape=jax.ShapeDtypeStruct((m, d), F32),
        scratch_shapes=[pltpu.VMEM((tm, d), BF16), pltpu.VMEM((tm, d), F32)],
        compiler_params=_params("parallel", "arbitrary"),
        name="mlp_final" if final_norm else "mlp",
    )(x, gain, mod, mod, mod, w_up, w_down, fin_gain)


def kernel(x, c, ada_w, ada_b, norm_mix, norm_mlp, fox_w_in, fox_b_f, fox_w_out, conv_w_in, conv_w,
           conv_w_out, mlp_w_up, mlp_w_down, final_norm):
    batch, seq, d = x.shape
    depth = ada_w.shape[0]
    dh = d // N_HEADS
    assert depth == 2 and dh == LANES and fox_w_in.shape[-1] == 3 * d + N_HEADS

    c_pad = jnp.pad(c, ((0, BF16_SUBLANES - batch), (0, 0)))
    mod = _ada_modulation(c_pad, ada_w, ada_b)
    mod = mod[:, :batch].reshape(depth, batch, N_MOD, 1, d)

    xf = x.reshape(batch * seq, d)
    row = lambda v: v.reshape(1, d)

    w_in = fox_w_in[0]
    w_f = jnp.pad(w_in[:, 3 * d:], ((0, 0), (0, LANES - N_HEADS))).astype(BF16)
    b_f = jnp.pad(fox_b_f[0], (0, LANES - N_HEADS)).reshape(1, LANES)
    qkv, f_logit = _inproj(xf, row(norm_mix[0]), mod, 0, w_in[:, :3 * d].astype(BF16), w_f,
                           seq=seq, n_scaled_cols=d, out_scale=dh ** -0.5)
    fcum = _forget_cumsum(f_logit, b_f, batch=batch, seq=seq)
    attn = _fox_attention(qkv, fcum, batch=batch, seq=seq, d=d)
    xf = _outproj(attn, fox_w_out[0].astype(BF16), xf, mod, 0, seq=seq)
    xf = _mlp(xf, row(norm_mlp[0]), mod, 0, mlp_w_up[0].astype(BF16), mlp_w_down[0].astype(BF16),
              row(final_norm), seq=seq, final_norm=False)

    proj = _inproj(xf, row(norm_mix[1]), mod, 1, conv_w_in[0].astype(BF16), None, seq=seq)[0]
    xf = _conv_outproj(proj, conv_w[0], conv_w_out[0].astype(BF16), xf, mod, 1, seq=seq)
    xf = _mlp(xf, row(norm_mlp[1]), mod, 1, mlp_w_up[1].astype(BF16), mlp_w_down[1].astype(BF16),
              row(final_norm), seq=seq, final_norm=True)
    return xf.reshape(batch, seq, d)
```

```python
import functools

import jax
import jax.numpy as jnp
from jax import lax
from jax.experimental import pallas as pl
from jax.experimental.pallas import tpu as pltpu

N_HEADS = 16
CONV_WIDTH = 3
N_MOD = 6
RMS_EPS = 1e-6
LOG2_E = 1.4426950408889634

LANES = 128
BF16_SUBLANES = 16
V7X_VMEM_BYTES = 64 * 1024 * 1024
VMEM_LIMIT_BYTES = V7X_VMEM_BYTES - 8 * 1024 * 1024

F32 = jnp.float32
BF16 = jnp.bfloat16


def _params(*semantics):
    return pltpu.CompilerParams(dimension_semantics=semantics, vmem_limit_bytes=VMEM_LIMIT_BYTES)


def _dot(a, b):
    return jnp.dot(a, b, preferred_element_type=F32)


def _norm_modulate(x, gain, shift, scale):
    inv = lax.rsqrt(jnp.mean(x * x, axis=-1, keepdims=True) + RMS_EPS)
    return (x * inv) * gain * (1.0 + scale) + shift


def _ada_kernel(c_ref, w_ref, b_ref, o_ref):
    c = c_ref[...]
    act = (c * jax.nn.sigmoid(c)).astype(BF16)
    o_ref[...] = _dot(act, w_ref[...].astype(BF16)) + b_ref[...]


def _ada_modulation(c_pad, ada_w, ada_b, tn=1024):
    depth, d, n = ada_w.shape
    rows = c_pad.shape[0]
    return pl.pallas_call(
        _ada_kernel,
        grid=(depth, n // tn),
        in_specs=[
            pl.BlockSpec((rows, d), lambda l, j: (0, 0)),
            pl.BlockSpec((None, d, tn), lambda l, j: (l, 0, j)),
            pl.BlockSpec((None, 1, tn), lambda l, j: (l, 0, j)),
        ],
        out_specs=pl.BlockSpec((None, rows, tn), lambda l, j: (l, 0, j)),
        out_shape=jax.ShapeDtypeStruct((depth, rows, n), F32),
        compiler_params=_params("parallel", "parallel"),
        name="ada_modulation",
    )(c_pad, ada_w, ada_b.reshape(depth, 1, n))


def _mod_spec(layer, which, tiles_per_batch, d):
    return pl.BlockSpec((None, None, None, 1, d),
                        lambda i, *_: (layer, i // tiles_per_batch, which, 0, 0))


def _inproj_kernel(*refs, n_scaled_tiles, out_scale, with_forget):
    if with_forget:
        x_ref, g_ref, sh_ref, sc_ref, w_ref, wf_ref, o_ref, f_ref, h_scr = refs
    else:
        x_ref, g_ref, sh_ref, sc_ref, w_ref, o_ref, h_scr = refs
    j = pl.program_id(1)

    @pl.when(j == 0)
    def _():
        h = _norm_modulate(x_ref[...], g_ref[...], sh_ref[...], sc_ref[...]).astype(BF16)
        h_scr[...] = h
        if with_forget:
            lane = lax.broadcasted_iota(jnp.int32, (1, LANES), 1)
            wf = jnp.where(lane < N_HEADS, wf_ref[...], 0.0).astype(BF16)
            f_ref[...] = _dot(h, wf)

    acc = _dot(h_scr[...], w_ref[...].astype(BF16))
    if n_scaled_tiles:
        acc = acc * jnp.where(j < n_scaled_tiles, out_scale, 1.0)
    o_ref[...] = acc.astype(o_ref.dtype)


def _inproj(x, gain, mod, layer, w, *, n, with_forget, seq, n_scaled_cols=0, out_scale=1.0,
            tm=1024, tn=1024):
    m, d = x.shape
    tpb = seq // tm
    in_specs = [
        pl.BlockSpec((tm, d), lambda i, j: (i, 0)),
        pl.BlockSpec((1, d), lambda i, j: (0, 0)),
        _mod_spec(layer, 0, tpb, d),
        _mod_spec(layer, 1, tpb, d),
        pl.BlockSpec((d, tn), lambda i, j: (0, j)),
    ]
    args = [x, gain, mod, mod, w]
    out_specs = [pl.BlockSpec((tm, tn), lambda i, j: (i, j))]
    out_shape = [jax.ShapeDtypeStruct((m, n), BF16)]
    if with_forget:
        in_specs.append(pl.BlockSpec((d, LANES), lambda i, j: (0, n // LANES)))
        args.append(w)
        out_specs.append(pl.BlockSpec((tm, LANES), lambda i, j: (i, 0)))
        out_shape.append(jax.ShapeDtypeStruct((m, LANES), F32))
    kern = functools.partial(_inproj_kernel, n_scaled_tiles=n_scaled_cols // tn,
                             out_scale=out_scale, with_forget=with_forget)
    return pl.pallas_call(
        kern,
        grid=(m // tm, n // tn),
        in_specs=in_specs,
        out_specs=out_specs,
        out_shape=out_shape,
        scratch_shapes=[pltpu.VMEM((tm, d), BF16)],
        compiler_params=_params("parallel", "arbitrary"),
        name="inproj_forget" if with_forget else "inproj",
    )(*args)


def _split3(x):
    hi = x.astype(BF16).astype(F32)
    r = x - hi
    mid = r.astype(BF16).astype(F32)
    lo = (r - mid).astype(BF16).astype(F32)
    return hi, mid, lo


def _forget_cumsum_kernel(fl_ref, bf_ref, o_ref, carry):
    @pl.when(pl.program_id(1) == 0)
    def _():
        carry[...] = jnp.zeros_like(carry)

    z = fl_ref[...] + bf_ref[...]
    log_f = jnp.minimum(z, 0.0) - jnp.log1p(jnp.exp(-jnp.abs(z)))
    tc = z.shape[0]
    row = lax.broadcasted_iota(jnp.int32, (tc, tc), 0)
    col = lax.broadcasted_iota(jnp.int32, (tc, tc), 1)
    tri = jnp.where(col <= row, 1.0, 0.0).astype(BF16)
    hi, mid, lo = _split3(log_f)
    csum = (_dot(tri, hi.astype(BF16)) + _dot(tri, mid.astype(BF16))) + _dot(tri, lo.astype(BF16))
    out = csum + carry[...]
    carry[...] = out[tc - 1:tc, :]
    neg_f = (out * -LOG2_E).T[0:N_HEADS, :]
    for plane, term in enumerate(_split3(neg_f)):
        o_ref[plane] = term


def _forget_cumsum(f_logit, b_f_pad, *, batch, seq, tc=512):
    fl = f_logit.reshape(batch, seq, LANES)
    return pl.pallas_call(
        _forget_cumsum_kernel,
        grid=(batch, seq // tc),
        in_specs=[
            pl.BlockSpec((None, tc, LANES), lambda b, i: (b, i, 0)),
            pl.BlockSpec((1, LANES), lambda b, i: (0, 0)),
        ],
        out_specs=pl.BlockSpec((None, 3, N_HEADS, tc), lambda b, i: (b, 0, 0, i)),
        out_shape=jax.ShapeDtypeStruct((batch, 3, N_HEADS, seq), F32),
        scratch_shapes=[pltpu.VMEM((1, LANES), F32)],
        compiler_params=_params("parallel", "arbitrary"),
        name="forget_cumsum",
    )(fl, b_f_pad)


def _fox_attn_kernel(q_ref, k_ref, v_ref, f_ref, o_ref, kaug, vt, *, tq, tk):
    seq, dh = k_ref.shape
    head = pl.ds(pl.program_id(1), 1)
    sub = lax.broadcasted_iota(jnp.int32, (8, 1), 0)
    lane = lax.broadcasted_iota(jnp.int32, (1, LANES), 1)

    for c in range(seq // tk):
        rows = slice(c * tk, (c + 1) * tk)
        f_rows = jnp.where(sub == 0, f_ref[0, head, rows], jnp.where(
            sub == 1, f_ref[1, head, rows], jnp.where(sub == 2, f_ref[2, head, rows], 0.0)))
        f_cols = jnp.concatenate([f_rows, jnp.zeros((LANES - 8, tk), F32)], axis=0).T
        kaug[rows, 0:dh] = k_ref[rows, :]
        kaug[rows, dh:dh + LANES] = f_cols.astype(BF16)
        vt[:, rows] = v_ref[rows, :].T

    ones3 = jnp.where(lane < 3, 1.0, 0.0).astype(BF16)
    key_in_chunk = lax.broadcasted_iota(jnp.int32, (tk, tq), 0)
    query_in_block = lax.broadcasted_iota(jnp.int32, (tk, tq), 1)
    blocks = [(qi, j) for qi in range(seq // tq) for j in range((qi + 1) * tq // tk)]

    def logits(qi, j):
        qa = jnp.concatenate([q_ref[qi * tq:(qi + 1) * tq, :],
                              jnp.broadcast_to(ones3, (tq, LANES))], axis=-1)
        return lax.dot_general(kaug[j * tk:(j + 1) * tk, :], qa, (((1,), (1,)), ((), ())),
                               preferred_element_type=F32)

    s_next = logits(*blocks[0])
    for t, (qi, j) in enumerate(blocks):
        s = s_next
        if t + 1 < len(blocks):
            s_next = logits(*blocks[t + 1])
        q_lo, k_lo = qi * tq, j * tk
        if j == 0:
            m = jnp.full((1, tq), -jnp.inf, F32)
            l = jnp.zeros((1, tq), F32)
            acc = jnp.zeros((dh, tq), F32)
        if k_lo + tk - 1 > q_lo:
            s = jnp.where(key_in_chunk + k_lo <= query_in_block + q_lo, s, -jnp.inf)
        m_new = jnp.maximum(m, jnp.max(s, axis=0, keepdims=True))
        alpha = jnp.exp2(m - m_new)
        p = jnp.exp2(s - m_new)
        l = alpha * l + jnp.sum(p, axis=0, keepdims=True)
        acc = alpha * acc + _dot(vt[:, k_lo:k_lo + tk], p.astype(BF16))
        m = m_new
        if k_lo + tk == q_lo + tq:
            o_ref[q_lo:q_lo + tq, :] = (acc * (1.0 / l)).T.astype(o_ref.dtype)


def _fox_attention(qkv, f_terms, *, batch, seq, d, tq=512, tk=512):
    dh = d // N_HEADS
    qkv3 = qkv.reshape(batch, seq, 3 * d)
    head_cols = lambda first: pl.BlockSpec((None, seq, dh), lambda b, h: (b, 0, first + h))
    return pl.pallas_call(
        functools.partial(_fox_attn_kernel, tq=tq, tk=tk),
        grid=(batch, N_HEADS),
        in_specs=[
            head_cols(0),
            head_cols(N_HEADS),
            head_cols(2 * N_HEADS),
            pl.BlockSpec((None, 3, N_HEADS, seq), lambda b, h: (b, 0, 0, 0)),
        ],
        out_specs=head_cols(0),
        out_shape=jax.ShapeDtypeStruct((batch, seq, d), BF16),
        scratch_shapes=[pltpu.VMEM((seq, dh + LANES), BF16), pltpu.VMEM((dh, seq), BF16)],
        compiler_params=_params("parallel", "parallel"),
        name="fox_attention",
    )(qkv3, qkv3, qkv3, f_terms).reshape(batch * seq, d)


def _outproj_kernel(a_ref, w_ref, x_ref, g_ref, o_ref):
    o_ref[...] = x_ref[...] + g_ref[...] * _dot(a_ref[...], w_ref[...])


def _outproj(a, w, x, mod, layer, *, seq, tm=512):
    m, d = x.shape
    tpb = seq // tm
    return pl.pallas_call(
        _outproj_kernel,
        grid=(m // tm,),
        in_specs=[
            pl.BlockSpec((tm, d), lambda i: (i, 0)),
            pl.BlockSpec((d, d), lambda i: (0, 0)),
            pl.BlockSpec((tm, d), lambda i: (i, 0)),
            _mod_spec(layer, 2, tpb, d),
        ],
        out_specs=pl.BlockSpec((tm, d), lambda i: (i, 0)),
        out_shape=jax.ShapeDtypeStruct((m, d), F32),
        compiler_params=_params("parallel"),
        name="attn_outproj",
    )(a, w, x, mod)


def _conv_outproj_kernel(b_ref, c_ref, u_ref, ch_ref, uh_ref, cw_ref, w_ref, x_ref, g_ref, o_ref,
                         ext, *, tiles_per_seq):
    tm = c_ref.shape[0]
    halo_rows = ch_ref.shape[0]
    seq_start = pl.program_id(0) % tiles_per_seq == 0
    halo = ch_ref[...].astype(F32) * uh_ref[...].astype(F32)
    ext[0:halo_rows, :] = jnp.where(seq_start, 0.0, halo)
    ext[halo_rows:halo_rows + tm, :] = c_ref[...].astype(F32) * u_ref[...].astype(F32)
    y = None
    for tap in range(CONV_WIDTH):
        start = halo_rows - (CONV_WIDTH - 1 - tap)
        term = cw_ref[tap:tap + 1, :] * ext[start:start + tm, :]
        y = term if y is None else y + term
    z = (b_ref[...].astype(F32) * y).astype(BF16)
    o_ref[...] = x_ref[...] + g_ref[...] * _dot(z, w_ref[...])


def _conv_outproj(proj, conv_w, w, x, mod, layer, *, seq, tm=512):
    m, d = x.shape
    tpb = seq // tm
    halo = BF16_SUBLANES
    halo_idx = lambda col: (lambda i: (jnp.maximum(i * (tm // halo) - 1, 0), col))
    return pl.pallas_call(
        functools.partial(_conv_outproj_kernel, tiles_per_seq=tpb),
        grid=(m // tm,),
        in_specs=[
            pl.BlockSpec((tm, d), lambda i: (i, 0)),
            pl.BlockSpec((tm, d), lambda i: (i, 1)),
            pl.BlockSpec((tm, d), lambda i: (i, 2)),
            pl.BlockSpec((halo, d), halo_idx(1)),
            pl.BlockSpec((halo, d), halo_idx(2)),
            pl.BlockSpec((CONV_WIDTH, d), lambda i: (0, 0)),
            pl.BlockSpec((d, d), lambda i: (0, 0)),
            pl.BlockSpec((tm, d), lambda i: (i, 0)),
            _mod_spec(layer, 2, tpb, d),
        ],
        out_specs=pl.BlockSpec((tm, d), lambda i: (i, 0)),
        out_shape=jax.ShapeDtypeStruct((m, d), F32),
        scratch_shapes=[pltpu.VMEM((halo + tm, d), F32)],
        compiler_params=_params("parallel"),
        name="conv_outproj",
    )(proj, proj, proj, proj, proj, conv_w, w, x, mod)


def _mlp_kernel(x_ref, g_ref, sh_ref, sc_ref, gate_ref, wu_ref, wd_ref, fin_ref, o_ref,
                h_scr, acc, *, final_norm):
    f = pl.program_id(1)

    @pl.when(f == 0)
    def _():
        h_scr[...] = _norm_modulate(x_ref[...], g_ref[...], sh_ref[...], sc_ref[...]).astype(BF16)
        acc[...] = jnp.zeros_like(acc)

    u = jnp.maximum(_dot(h_scr[...], wu_ref[...]), 0.0)
    acc[...] += _dot((u * u).astype(BF16), wd_ref[...])

    @pl.when(f == pl.num_programs(1) - 1)
    def _():
        y = x_ref[...] + gate_ref[...] * acc[...]
        if final_norm:
            inv = lax.rsqrt(jnp.mean(y * y, axis=-1, keepdims=True) + RMS_EPS)
            y = (y * inv) * fin_ref[...]
        o_ref[...] = y


def _mlp(x, gain, mod, layer, w_up, w_down, fin_gain, *, seq, final_norm, tm=512, tf=1024):
    m, d = x.shape
    dff = w_up.shape[2]
    tpb = seq // tm
    return pl.pallas_call(
        functools.partial(_mlp_kernel, final_norm=final_norm),
        grid=(m // tm, dff // tf),
        in_specs=[
            pl.BlockSpec((tm, d), lambda i, f: (i, 0)),
            pl.BlockSpec((1, d), lambda i, f: (0, 0)),
            _mod_spec(layer, 3, tpb, d),
            _mod_spec(layer, 4, tpb, d),
            _mod_spec(layer, 5, tpb, d),
            pl.BlockSpec((None, d, tf), lambda i, f: (layer, 0, f)),
            pl.BlockSpec((None, tf, d), lambda i, f: (layer, f, 0)),
            pl.BlockSpec((1, d), lambda i, f: (0, 0)),
        ],
        out_specs=pl.BlockSpec((tm, d), lambda i, f: (i, 0)),
        out_shape=jax.ShapeDtypeStruct((m, d), F32),
        scratch_shapes=[pltpu.VMEM((tm, d), BF16), pltpu.VMEM((tm, d), F32)],
        compiler_params=_params("parallel", "arbitrary"),
        name="mlp_final" if final_norm else "mlp",
    )(x, gain, mod, mod, mod, w_up, w_down, fin_gain)


def kernel(x, c, ada_w, ada_b, norm_mix, norm_mlp, fox_w_in, fox_b_f, fox_w_out, conv_w_in, conv_w,
           conv_w_out, mlp_w_up, mlp_w_down, final_norm):
    batch, seq, d = x.shape
    depth = ada_w.shape[0]
    dh = d // N_HEADS
    assert depth == 2 and dh == LANES and fox_w_in.shape[-1] == 3 * d + N_HEADS

    c_pad = jnp.pad(c, ((0, BF16_SUBLANES - batch), (0, 0)))
    mod = _ada_modulation(c_pad, ada_w, ada_b)
    mod = mod[:, :batch].reshape(depth, batch, N_MOD, 1, d)

    xf = x.reshape(batch * seq, d)
    row = lambda v: v.reshape(1, d)

    b_f = jnp.pad(fox_b_f[0], (0, LANES - N_HEADS)).reshape(1, LANES)
    qkv, f_logit = _inproj(xf, row(norm_mix[0]), mod, 0, fox_w_in[0], n=3 * d, with_forget=True,
                           seq=seq, n_scaled_cols=d, out_scale=LOG2_E * dh ** -0.5)
    fcum = _forget_cumsum(f_logit, b_f, batch=batch, seq=seq)
    attn = _fox_attention(qkv, fcum, batch=batch, seq=seq, d=d)
    xf = _outproj(attn, fox_w_out[0].astype(BF16), xf, mod, 0, seq=seq)
    w_up, w_down = mlp_w_up.astype(BF16), mlp_w_down.astype(BF16)
    xf = _mlp(xf, row(norm_mlp[0]), mod, 0, w_up, w_down, row(final_norm), seq=seq,
              final_norm=False)

    proj = _inproj(xf, row(norm_mix[1]), mod, 1, conv_w_in[0], n=3 * d, with_forget=False,
                   seq=seq)[0]
    xf = _conv_outproj(proj, conv_w[0], conv_w_out[0].astype(BF16), xf, mod, 1, seq=seq)
    xf = _mlp(xf, row(norm_mlp[1]), mod, 1, w_up, w_down, row(final_norm), seq=seq,
              final_norm=True)
    return xf.reshape(batch, seq, d)
```

```python
import functools

import jax
import jax.numpy as jnp
from jax import lax
from jax.experimental import pallas as pl
from jax.experimental.pallas import tpu as pltpu

N_HEADS = 16
CONV_WIDTH = 3
N_MOD = 6
RMS_EPS = 1e-6
LOG2_E = 1.4426950408889634

LANES = 128
BF16_SUBLANES = 16
V7X_VMEM_BYTES = 64 * 1024 * 1024
VMEM_LIMIT_BYTES = V7X_VMEM_BYTES - 8 * 1024 * 1024

F32 = jnp.float32
BF16 = jnp.bfloat16


def _params(*semantics):
    return pltpu.CompilerParams(dimension_semantics=semantics, vmem_limit_bytes=VMEM_LIMIT_BYTES)


def _dot(a, b):
    return jnp.dot(a, b, preferred_element_type=F32)


def _norm_modulate(x, gain, shift, scale):
    inv = lax.rsqrt(jnp.mean(x * x, axis=-1, keepdims=True) + RMS_EPS)
    return (x * inv) * gain * (1.0 + scale) + shift


def _ada_kernel(c_ref, w_ref, b_ref, o_ref):
    c = c_ref[...]
    act = (c * jax.nn.sigmoid(c)).astype(BF16)
    o_ref[...] = _dot(act, w_ref[...].astype(BF16)) + b_ref[...]


def _ada_modulation(c_pad, ada_w, ada_b, tn=1024):
    depth, d, n = ada_w.shape
    rows = c_pad.shape[0]
    return pl.pallas_call(
        _ada_kernel,
        grid=(depth, n // tn),
        in_specs=[
            pl.BlockSpec((rows, d), lambda l, j: (0, 0)),
            pl.BlockSpec((None, d, tn), lambda l, j: (l, 0, j)),
            pl.BlockSpec((None, 1, tn), lambda l, j: (l, 0, j)),
        ],
        out_specs=pl.BlockSpec((None, rows, tn), lambda l, j: (l, 0, j)),
        out_shape=jax.ShapeDtypeStruct((depth, rows, n), F32),
        compiler_params=_params("parallel", "parallel"),
        name="ada_modulation",
    )(c_pad, ada_w, ada_b.reshape(depth, 1, n))


def _mod_spec(layer, which, tiles_per_batch, d, ahead=0, last_tile=None):
    def index(i, *_):
        tile = jnp.minimum(i + ahead, last_tile) if ahead else i
        return (layer, tile // tiles_per_batch, which, 0, 0)
    return pl.BlockSpec((None, None, None, 1, d), index)


def _inproj_kernel(*refs, n_scaled_cols, out_scale, with_forget):
    if with_forget:
        (x0_ref, xn_ref, g_ref, sh0_ref, sc0_ref, shn_ref, scn_ref, w_ref, wf_ref,
         o_ref, f_ref, h_scr) = refs
    else:
        x0_ref, xn_ref, g_ref, sh0_ref, sc0_ref, shn_ref, scn_ref, w_ref, o_ref, h_scr = refs
    i = pl.program_id(0)
    slot = i % 2

    @pl.when(i == 0)
    def _():
        h_scr[0] = _norm_modulate(x0_ref[...], g_ref[...], sh0_ref[...], sc0_ref[...]).astype(BF16)

    tm, d = xn_ref.shape
    n_groups = o_ref.shape[1] // d
    rows_per_slice = tm // (n_groups - 1)
    for grp in range(n_groups):
        cols = slice(grp * d, (grp + 1) * d)
        acc = _dot(h_scr[slot], w_ref[:, cols])
        if grp * d < n_scaled_cols:
            acc = acc * out_scale
        o_ref[:, cols] = acc.astype(o_ref.dtype)
        if grp < n_groups - 1:
            rows = slice(grp * rows_per_slice, (grp + 1) * rows_per_slice)
            h_scr[1 - slot, rows, :] = _norm_modulate(xn_ref[rows, :], g_ref[...], shn_ref[...],
                                                      scn_ref[...]).astype(BF16)
    if with_forget:
        lane = lax.broadcasted_iota(jnp.int32, (1, LANES), 1)
        f_ref[...] = _dot(h_scr[slot], jnp.where(lane < N_HEADS, wf_ref[...], 0.0).astype(BF16))


def _inproj(x, gain, mod, layer, w, *, n, with_forget, seq, n_scaled_cols=0, out_scale=1.0, tm=256):
    m, d = x.shape
    tpb = seq // tm
    last = m // tm - 1
    in_specs = [
        pl.BlockSpec((tm, d), lambda i: (0, 0)),
        pl.BlockSpec((tm, d), lambda i: (jnp.minimum(i + 1, last), 0)),
        pl.BlockSpec((1, d), lambda i: (0, 0)),
        _mod_spec(layer, 0, tpb, d),
        _mod_spec(layer, 1, tpb, d),
        _mod_spec(layer, 0, tpb, d, ahead=1, last_tile=last),
        _mod_spec(layer, 1, tpb, d, ahead=1, last_tile=last),
        pl.BlockSpec((d, n), lambda i: (0, 0), pipeline_mode=pl.Buffered(1)),
    ]
    args = [x, x, gain, mod, mod, mod, mod, w]
    out_specs = [pl.BlockSpec((tm, n), lambda i: (i, 0))]
    out_shape = [jax.ShapeDtypeStruct((m, n), BF16)]
    if with_forget:
        in_specs.append(pl.BlockSpec((d, LANES), lambda i: (0, n // LANES)))
        args.append(w)
        out_specs.append(pl.BlockSpec((tm, LANES), lambda i: (i, 0)))
        out_shape.append(jax.ShapeDtypeStruct((m, LANES), F32))
    kern = functools.partial(_inproj_kernel, n_scaled_cols=n_scaled_cols, out_scale=out_scale,
                             with_forget=with_forget)
    return pl.pallas_call(
        kern,
        grid=(m // tm,),
        in_specs=in_specs,
        out_specs=out_specs,
        out_shape=out_shape,
        scratch_shapes=[pltpu.VMEM((2, tm, d), BF16)],
        compiler_params=_params("arbitrary"),
        name="inproj_forget" if with_forget else "inproj",
    )(*args)


def _split3(x):
    hi = x.astype(BF16).astype(F32)
    r = x - hi
    mid = r.astype(BF16).astype(F32)
    lo = (r - mid).astype(BF16).astype(F32)
    return hi, mid, lo


def _forget_cumsum_kernel(fl_ref, bf_ref, o_ref, carry):
    @pl.when(pl.program_id(1) == 0)
    def _():
        carry[...] = jnp.zeros_like(carry)

    z = fl_ref[...] + bf_ref[...]
    log_f = jnp.minimum(z, 0.0) - jnp.log1p(jnp.exp(-jnp.abs(z)))
    tc = z.shape[0]
    row = lax.broadcasted_iota(jnp.int32, (tc, tc), 0)
    col = lax.broadcasted_iota(jnp.int32, (tc, tc), 1)
    tri = jnp.where(col <= row, 1.0, 0.0).astype(BF16)
    hi, mid, lo = _split3(log_f)
    csum = (_dot(tri, hi.astype(BF16)) + _dot(tri, mid.astype(BF16))) + _dot(tri, lo.astype(BF16))
    out = csum + carry[...]
    carry[...] = out[tc - 1:tc, :]
    neg_f = (out * -LOG2_E).T[0:N_HEADS, :]
    for plane, term in enumerate(_split3(neg_f)):
        o_ref[plane] = term


def _forget_cumsum(f_logit, b_f_pad, *, batch, seq, tc=512):
    fl = f_logit.reshape(batch, seq, LANES)
    return pl.pallas_call(
        _forget_cumsum_kernel,
        grid=(batch, seq // tc),
        in_specs=[
            pl.BlockSpec((None, tc, LANES), lambda b, i: (b, i, 0)),
            pl.BlockSpec((1, LANES), lambda b, i: (0, 0)),
        ],
        out_specs=pl.BlockSpec((None, 3, N_HEADS, tc), lambda b, i: (b, 0, 0, i)),
        out_shape=jax.ShapeDtypeStruct((batch, 3, N_HEADS, seq), F32),
        scratch_shapes=[pltpu.VMEM((1, LANES), F32)],
        compiler_params=_params("parallel", "arbitrary"),
        name="forget_cumsum",
    )(fl, b_f_pad)


def _fox_attn_kernel(q_ref, k_ref, v_ref, f_ref, o_ref, kaug, vt, *, tq, tk):
    seq, dh = k_ref.shape
    head = pl.ds(pl.program_id(1), 1)
    sub = lax.broadcasted_iota(jnp.int32, (8, 1), 0)
    lane = lax.broadcasted_iota(jnp.int32, (1, LANES), 1)

    for c in range(seq // tk):
        rows = slice(c * tk, (c + 1) * tk)
        f_rows = jnp.where(sub == 0, f_ref[0, head, rows], jnp.where(
            sub == 1, f_ref[1, head, rows], jnp.where(sub == 2, f_ref[2, head, rows], 0.0)))
        f_cols = jnp.concatenate([f_rows, jnp.zeros((LANES - 8, tk), F32)], axis=0).T
        kaug[rows, 0:dh] = k_ref[rows, :]
        kaug[rows, dh:dh + LANES] = f_cols.astype(BF16)
        vt[:, rows] = v_ref[rows, :].T

    ones3 = jnp.where(lane < 3, 1.0, 0.0).astype(BF16)
    key_in_chunk = lax.broadcasted_iota(jnp.int32, (tk, tq), 0)
    query_in_block = lax.broadcasted_iota(jnp.int32, (tk, tq), 1)
    blocks = [(qi, j) for qi in range(seq // tq) for j in range((qi + 1) * tq // tk)]

    def logits(qi, j):
        qa = jnp.concatenate([q_ref[qi * tq:(qi + 1) * tq, :],
                              jnp.broadcast_to(ones3, (tq, LANES))], axis=-1)
        return lax.dot_general(kaug[j * tk:(j + 1) * tk, :], qa, (((1,), (1,)), ((), ())),
                               preferred_element_type=F32)

    s_next = logits(*blocks[0])
    for t, (qi, j) in enumerate(blocks):
        s = s_next
        if t + 1 < len(blocks):
            s_next = logits(*blocks[t + 1])
        q_lo, k_lo = qi * tq, j * tk
        if j == 0:
            m = jnp.full((1, tq), -jnp.inf, F32)
            l = jnp.zeros((1, tq), F32)
            acc = jnp.zeros((dh, tq), F32)
        if k_lo + tk - 1 > q_lo:
            s = jnp.where(key_in_chunk + k_lo <= query_in_block + q_lo, s, -jnp.inf)
        m_new = jnp.maximum(m, jnp.max(s, axis=0, keepdims=True))
        alpha = jnp.exp2(m - m_new)
        p = jnp.exp2(s - m_new)
        l = alpha * l + jnp.sum(p, axis=0, keepdims=True)
        acc = alpha * acc + _dot(vt[:, k_lo:k_lo + tk], p.astype(BF16))
        m = m_new
        if k_lo + tk == q_lo + tq:
            o_ref[q_lo:q_lo + tq, :] = (acc * (1.0 / l)).T.astype(o_ref.dtype)


def _fox_attention(qkv, f_terms, *, batch, seq, d, tq=512, tk=512):
    dh = d // N_HEADS
    qkv3 = qkv.reshape(batch, seq, 3 * d)
    head_cols = lambda first: pl.BlockSpec((None, seq, dh), lambda b, h: (b, 0, first + h))
    return pl.pallas_call(
        functools.partial(_fox_attn_kernel, tq=tq, tk=tk),
        grid=(batch, N_HEADS),
        in_specs=[
            head_cols(0),
            head_cols(N_HEADS),
            head_cols(2 * N_HEADS),
            pl.BlockSpec((None, 3, N_HEADS, seq), lambda b, h: (b, 0, 0, 0)),
        ],
        out_specs=head_cols(0),
        out_shape=jax.ShapeDtypeStruct((batch, seq, d), BF16),
        scratch_shapes=[pltpu.VMEM((seq, dh + LANES), BF16), pltpu.VMEM((dh, seq), BF16)],
        compiler_params=_params("parallel", "parallel"),
        name="fox_attention",
    )(qkv3, qkv3, qkv3, f_terms).reshape(batch * seq, d)


def _outproj_kernel(a_ref, w_ref, x_ref, g_ref, o_ref):
    o_ref[...] = x_ref[...] + g_ref[...] * _dot(a_ref[...], w_ref[...])


def _outproj(a, w, x, mod, layer, *, seq, tm=512):
    m, d = x.shape
    tpb = seq // tm
    return pl.pallas_call(
        _outproj_kernel,
        grid=(m // tm,),
        in_specs=[
            pl.BlockSpec((tm, d), lambda i: (i, 0)),
            pl.BlockSpec((d, d), lambda i: (0, 0)),
            pl.BlockSpec((tm, d), lambda i: (i, 0)),
            _mod_spec(layer, 2, tpb, d),
        ],
        out_specs=pl.BlockSpec((tm, d), lambda i: (i, 0)),
        out_shape=jax.ShapeDtypeStruct((m, d), F32),
        compiler_params=_params("parallel"),
        name="attn_outproj",
    )(a, w, x, mod)


def _conv_outproj_kernel(b_ref, c_ref, u_ref, ch_ref, uh_ref, cw_ref, w_ref, x_ref, g_ref, o_ref,
                         ext, *, tiles_per_seq):
    tm = c_ref.shape[0]
    halo_rows = ch_ref.shape[0]
    seq_start = pl.program_id(0) % tiles_per_seq == 0
    halo = ch_ref[...].astype(F32) * uh_ref[...].astype(F32)
    ext[0:halo_rows, :] = jnp.where(seq_start, 0.0, halo)
    ext[halo_rows:halo_rows + tm, :] = c_ref[...].astype(F32) * u_ref[...].astype(F32)
    y = None
    for tap in range(CONV_WIDTH):
        start = halo_rows - (CONV_WIDTH - 1 - tap)
        term = cw_ref[tap:tap + 1, :] * ext[start:start + tm, :]
        y = term if y is None else y + term
    z = (b_ref[...].astype(F32) * y).astype(BF16)
    o_ref[...] = x_ref[...] + g_ref[...] * _dot(z, w_ref[...])


def _conv_outproj(proj, conv_w, w, x, mod, layer, *, seq, tm=512):
    m, d = x.shape
    tpb = seq // tm
    halo = BF16_SUBLANES
    halo_idx = lambda col: (lambda i: (jnp.maximum(i * (tm // halo) - 1, 0), col))
    return pl.pallas_call(
        functools.partial(_conv_outproj_kernel, tiles_per_seq=tpb),
        grid=(m // tm,),
        in_specs=[
            pl.BlockSpec((tm, d), lambda i: (i, 0)),
            pl.BlockSpec((tm, d), lambda i: (i, 1)),
            pl.BlockSpec((tm, d), lambda i: (i, 2)),
            pl.BlockSpec((halo, d), halo_idx(1)),
            pl.BlockSpec((halo, d), halo_idx(2)),
            pl.BlockSpec((CONV_WIDTH, d), lambda i: (0, 0)),
            pl.BlockSpec((d, d), lambda i: (0, 0)),
            pl.BlockSpec((tm, d), lambda i: (i, 0)),
            _mod_spec(layer, 2, tpb, d),
        ],
        out_specs=pl.BlockSpec((tm, d), lambda i: (i, 0)),
        out_shape=jax.ShapeDtypeStruct((m, d), F32),
        scratch_shapes=[pltpu.VMEM((halo + tm, d), F32)],
        compiler_params=_params("parallel"),
        name="conv_outproj",
    )(proj, proj, proj, proj, proj, conv_w, w, x, mod)


def _mlp_kernel(x_ref, xn_ref, g_ref, sh_ref, sc_ref, shn_ref, scn_ref, gate_ref, wu_ref, wd_ref,
                fin_ref, o_ref, h_scr, acc, *, final_norm):
    i = pl.program_id(0)
    f = pl.program_id(1)
    slot = i % 2
    tm = x_ref.shape[0]
    rows_per_step = tm // pl.num_programs(1)

    @pl.when(jnp.logical_and(i == 0, f == 0))
    def _():
        h_scr[0] = _norm_modulate(x_ref[...], g_ref[...], sh_ref[...], sc_ref[...]).astype(BF16)

    @pl.when(f == 0)
    def _():
        acc[...] = jnp.zeros_like(acc)

    half = wu_ref.shape[1] // 2
    u_lo = jnp.maximum(_dot(h_scr[slot], wu_ref[:, 0:half]), 0.0)
    rows = pl.ds(pl.multiple_of(f * rows_per_step, rows_per_step), rows_per_step)
    h_scr[1 - slot, rows, :] = _norm_modulate(xn_ref[rows, :], g_ref[...], shn_ref[...],
                                              scn_ref[...]).astype(BF16)
    u_hi = jnp.maximum(_dot(h_scr[slot], wu_ref[:, half:]), 0.0)
    u = jnp.concatenate([u_lo, u_hi], axis=1)
    acc[...] += _dot((u * u).astype(BF16), wd_ref[...])

    @pl.when(f == pl.num_programs(1) - 1)
    def _():
        y = x_ref[...] + gate_ref[...] * acc[...]
        if final_norm:
            inv = lax.rsqrt(jnp.mean(y * y, axis=-1, keepdims=True) + RMS_EPS)
            y = (y * inv) * fin_ref[...]
        o_ref[...] = y


def _mlp(x, gain, mod, layer, w_up, w_down, fin_gain, *, seq, final_norm, tm=512, tf=1024):
    m, d = x.shape
    dff = w_up.shape[2]
    tpb = seq // tm
    last = m // tm - 1
    assert (tm // (dff // tf)) % BF16_SUBLANES == 0
    return pl.pallas_call(
        functools.partial(_mlp_kernel, final_norm=final_norm),
        grid=(m // tm, dff // tf),
        in_specs=[
            pl.BlockSpec((tm, d), lambda i, f: (i, 0)),
            pl.BlockSpec((tm, d), lambda i, f: (jnp.minimum(i + 1, last), 0)),
            pl.BlockSpec((1, d), lambda i, f: (0, 0)),
            _mod_spec(layer, 3, tpb, d),
            _mod_spec(layer, 4, tpb, d),
            _mod_spec(layer, 3, tpb, d, ahead=1, last_tile=last),
            _mod_spec(layer, 4, tpb, d, ahead=1, last_tile=last),
            _mod_spec(layer, 5, tpb, d),
            pl.BlockSpec((None, d, tf), lambda i, f: (layer, 0, f)),
            pl.BlockSpec((None, tf, d), lambda i, f: (layer, f, 0)),
            pl.BlockSpec((1, d), lambda i, f: (0, 0)),
        ],
        out_specs=pl.BlockSpec((tm, d), lambda i, f: (i, 0)),
        out_shape=jax.ShapeDtypeStruct((m, d), F32),
        scratch_shapes=[pltpu.VMEM((2, tm, d), BF16), pltpu.VMEM((tm, d), F32)],
        compiler_params=_params("arbitrary", "arbitrary"),
        name="mlp_final" if final_norm else "mlp",
    )(x, x, gain, mod, mod, mod, mod, mod, w_up, w_down, fin_gain)


def kernel(x, c, ada_w, ada_b, norm_mix, norm_mlp, fox_w_in, fox_b_f, fox_w_out, conv_w_in, conv_w,
           conv_w_out, mlp_w_up, mlp_w_down, final_norm):
    batch, seq, d = x.shape
    depth = ada_w.shape[0]
    dh = d // N_HEADS
    assert depth == 2 and dh == LANES and fox_w_in.shape[-1] == 3 * d + N_HEADS

    c_pad = jnp.pad(c, ((0, BF16_SUBLANES - batch), (0, 0)))
    mod = _ada_modulation(c_pad, ada_w, ada_b)
    mod = mod[:, :batch].reshape(depth, batch, N_MOD, 1, d)

    xf = x.reshape(batch * seq, d)
    row = lambda v: v.reshape(1, d)

    b_f = jnp.pad(fox_b_f[0], (0, LANES - N_HEADS)).reshape(1, LANES)
    qkv, f_logit = _inproj(xf, row(norm_mix[0]), mod, 0, fox_w_in[0].astype(BF16), n=3 * d,
                           with_forget=True, seq=seq, n_scaled_cols=d,
                           out_scale=LOG2_E * dh ** -0.5)
    fcum = _forget_cumsum(f_logit, b_f, batch=batch, seq=seq)
    attn = _fox_attention(qkv, fcum, batch=batch, seq=seq, d=d)
    xf = _outproj(attn, fox_w_out[0].astype(BF16), xf, mod, 0, seq=seq)
    w_up, w_down = mlp_w_up.astype(BF16), mlp_w_down.astype(BF16)
    xf = _mlp(xf, row(norm_mlp[0]), mod, 0, w_up, w_down, row(final_norm), seq=seq,
              final_norm=False)

    proj = _inproj(xf, row(norm_mix[1]), mod, 1, conv_w_in[0].astype(BF16), n=3 * d,
                   with_forget=False, seq=seq)[0]
    xf = _conv_outproj(proj, conv_w[0], conv_w_out[0].astype(BF16), xf, mod, 1, seq=seq)
    xf = _mlp(xf, row(norm_mlp[1]), mod, 1, w_up, w_down, row(final_norm), seq=seq,
              final_norm=True)
    return xf.reshape(batch, seq, d)
```

```python
import functools

import jax
import jax.numpy as jnp
from jax import lax
from jax.experimental import pallas as pl
from jax.experimental.pallas import tpu as pltpu

N_HEADS = 16
CONV_WIDTH = 3
N_MOD = 6
RMS_EPS = 1e-6
LOG2_E = 1.4426950408889634
LOOKAHEAD = 1

LANES = 128
BF16_SUBLANES = 16
V7X_VMEM_BYTES = 64 * 1024 * 1024
VMEM_LIMIT_BYTES = V7X_VMEM_BYTES - 8 * 1024 * 1024

F32 = jnp.float32
BF16 = jnp.bfloat16


def _params(*semantics):
    return pltpu.CompilerParams(dimension_semantics=semantics, vmem_limit_bytes=VMEM_LIMIT_BYTES)


def _dot(a, b):
    return jnp.dot(a, b, preferred_element_type=F32)


def _norm_modulate(x, gain, shift, scale):
    inv = lax.rsqrt(jnp.mean(x * x, axis=-1, keepdims=True) + RMS_EPS)
    return (x * inv) * gain * (1.0 + scale) + shift


def _ada_kernel(c_ref, w_ref, b_ref, o_ref):
    c = c_ref[...]
    act = (c * jax.nn.sigmoid(c)).astype(BF16)
    o_ref[...] = _dot(act, w_ref[...].astype(BF16)) + b_ref[...]


def _ada_modulation(c_pad, ada_w, ada_b, tn=1024):
    depth, d, n = ada_w.shape
    rows = c_pad.shape[0]
    return pl.pallas_call(
        _ada_kernel,
        grid=(depth, n // tn),
        in_specs=[
            pl.BlockSpec((rows, d), lambda l, j: (0, 0)),
            pl.BlockSpec((None, d, tn), lambda l, j: (l, 0, j)),
            pl.BlockSpec((None, 1, tn), lambda l, j: (l, 0, j)),
        ],
        out_specs=pl.BlockSpec((None, rows, tn), lambda l, j: (l, 0, j)),
        out_shape=jax.ShapeDtypeStruct((depth, rows, n), F32),
        compiler_params=_params("parallel", "parallel"),
        name="ada_modulation",
    )(c_pad, ada_w, ada_b.reshape(depth, 1, n))


def _mod_spec(layer, which, tiles_per_batch, d, ahead=0, last_tile=None):
    def index(i, *_):
        tile = jnp.minimum(i + ahead, last_tile) if ahead else i
        return (layer, tile // tiles_per_batch, which, 0, 0)
    return pl.BlockSpec((None, None, None, 1, d), index)


def _inproj_kernel(*refs, n_scaled_cols, out_scale, with_forget, n_casts):
    x0_ref, xn_ref, g_ref, sh0_ref, sc0_ref, shn_ref, scn_ref, w_ref = refs[:8]
    refs = refs[8:]
    if with_forget:
        wf_ref, refs = refs[0], refs[1:]
    cast_in, refs = refs[:n_casts], refs[n_casts:]
    o_ref, refs = refs[0], refs[1:]
    if with_forget:
        f_ref, refs = refs[0], refs[1:]
    cast_out, (h_scr,) = refs[:n_casts], refs[n_casts:]
    for src, dst in zip(cast_in, cast_out):
        dst[...] = src[...].astype(dst.dtype)
    i = pl.program_id(0)
    slot = i % 2

    @pl.when(i == 0)
    def _():
        h_scr[0] = _norm_modulate(x0_ref[...], g_ref[...], sh0_ref[...], sc0_ref[...]).astype(BF16)

    tm, d = xn_ref.shape
    n_groups = o_ref.shape[1] // d
    rows_per_slice = tm // (n_groups - 1)
    for grp in range(n_groups):
        cols = slice(grp * d, (grp + 1) * d)
        acc = _dot(h_scr[slot], w_ref[:, cols])
        if grp * d < n_scaled_cols:
            acc = acc * out_scale
        o_ref[:, cols] = acc.astype(o_ref.dtype)
        if grp < n_groups - 1:
            rows = slice(grp * rows_per_slice, (grp + 1) * rows_per_slice)
            h_scr[1 - slot, rows, :] = _norm_modulate(xn_ref[rows, :], g_ref[...], shn_ref[...],
                                                      scn_ref[...]).astype(BF16)
    if with_forget:
        lane = lax.broadcasted_iota(jnp.int32, (1, LANES), 1)
        f_ref[...] = _dot(h_scr[slot], jnp.where(lane < N_HEADS, wf_ref[...], 0.0).astype(BF16))


def _inproj(x, gain, mod, layer, w, *, n, with_forget, seq, n_scaled_cols=0, out_scale=1.0, tm=256,
            casts=()):
    m, d = x.shape
    tpb = seq // tm
    last = m // tm - 1
    steps = m // tm
    assert n % d == 0 and n_scaled_cols % d == 0
    slab = lambda a: pl.BlockSpec((a.shape[0] // steps, a.shape[1]), lambda i: (i, 0))
    in_specs = [
        pl.BlockSpec((tm, d), lambda i: (0, 0)),
        pl.BlockSpec((tm, d), lambda i: (jnp.minimum(i + 1, last), 0)),
        pl.BlockSpec((1, d), lambda i: (0, 0)),
        _mod_spec(layer, 0, tpb, d),
        _mod_spec(layer, 1, tpb, d),
        _mod_spec(layer, 0, tpb, d, ahead=1, last_tile=last),
        _mod_spec(layer, 1, tpb, d, ahead=1, last_tile=last),
        pl.BlockSpec((d, n), lambda i: (0, 0), pipeline_mode=pl.Buffered(1)),
    ]
    args = [x, x, gain, mod, mod, mod, mod, w]
    out_specs = [pl.BlockSpec((tm, n), lambda i: (i, 0))]
    out_shape = [jax.ShapeDtypeStruct((m, n), BF16)]
    if with_forget:
        in_specs.append(pl.BlockSpec((d, LANES), lambda i: (0, n // LANES)))
        args.append(w)
        out_specs.append(pl.BlockSpec((tm, LANES), lambda i: (i, 0)))
        out_shape.append(jax.ShapeDtypeStruct((m, LANES), F32))
    for a in casts:
        assert (a.shape[0] // steps) % BF16_SUBLANES == 0
        in_specs.append(slab(a))
        args.append(a)
        out_specs.append(slab(a))
        out_shape.append(jax.ShapeDtypeStruct(a.shape, BF16))
    kern = functools.partial(_inproj_kernel, n_scaled_cols=n_scaled_cols, out_scale=out_scale,
                             with_forget=with_forget, n_casts=len(casts))
    return pl.pallas_call(
        kern,
        grid=(m // tm,),
        in_specs=in_specs,
        out_specs=out_specs,
        out_shape=out_shape,
        scratch_shapes=[pltpu.VMEM((2, tm, d), BF16)],
        compiler_params=_params("arbitrary"),
        name="inproj_forget" if with_forget else "inproj",
    )(*args)


def _split3(x):
    hi = x.astype(BF16).astype(F32)
    r = x - hi
    mid = r.astype(BF16).astype(F32)
    lo = (r - mid).astype(BF16).astype(F32)
    return hi, mid, lo


def _forget_cumsum_kernel(fl_ref, bf_ref, o_ref, carry):
    @pl.when(pl.program_id(1) == 0)
    def _():
        carry[...] = jnp.zeros_like(carry)

    z = fl_ref[...] + bf_ref[...]
    log_f = jnp.minimum(z, 0.0) - jnp.log1p(jnp.exp(-jnp.abs(z)))
    tc = z.shape[0]
    row = lax.broadcasted_iota(jnp.int32, (tc, tc), 0)
    col = lax.broadcasted_iota(jnp.int32, (tc, tc), 1)
    tri = jnp.where(col <= row, 1.0, 0.0).astype(BF16)
    hi, mid, lo = _split3(log_f)
    csum = (_dot(tri, hi.astype(BF16)) + _dot(tri, mid.astype(BF16))) + _dot(tri, lo.astype(BF16))
    out = csum + carry[...]
    carry[...] = out[tc - 1:tc, :]
    neg_f = (out * -LOG2_E).T[0:N_HEADS, :]
    for plane, term in enumerate(_split3(neg_f)):
        o_ref[plane] = term


def _forget_cumsum(f_logit, b_f_pad, *, batch, seq, tc=512):
    fl = f_logit.reshape(batch, seq, LANES)
    return pl.pallas_call(
        _forget_cumsum_kernel,
        grid=(batch, seq // tc),
        in_specs=[
            pl.BlockSpec((None, tc, LANES), lambda b, i: (b, i, 0)),
            pl.BlockSpec((1, LANES), lambda b, i: (0, 0)),
        ],
        out_specs=pl.BlockSpec((None, 3, N_HEADS, tc), lambda b, i: (b, 0, 0, i)),
        out_shape=jax.ShapeDtypeStruct((batch, 3, N_HEADS, seq), F32),
        scratch_shapes=[pltpu.VMEM((1, LANES), F32)],
        compiler_params=_params("parallel", "arbitrary"),
        name="forget_cumsum",
    )(fl, b_f_pad)


def _fox_attn_kernel(q_ref, k_ref, v_ref, f_ref, wu_ref, wd_ref, o_ref, wu_bf_ref, wd_bf_ref,
                     kaug, vt, *, tq, tk):
    wu_bf_ref[...] = wu_ref[...].astype(BF16)
    wd_bf_ref[...] = wd_ref[...].astype(BF16)
    seq, dh = k_ref.shape
    head = pl.ds(pl.program_id(1), 1)
    sub = lax.broadcasted_iota(jnp.int32, (8, 1), 0)
    lane = lax.broadcasted_iota(jnp.int32, (1, LANES), 1)

    for c in range(seq // tk):
        rows = slice(c * tk, (c + 1) * tk)
        f_rows = jnp.where(sub == 0, f_ref[0, head, rows], jnp.where(
            sub == 1, f_ref[1, head, rows], jnp.where(sub == 2, f_ref[2, head, rows], 0.0)))
        f_cols = jnp.concatenate([f_rows, jnp.zeros((LANES - 8, tk), F32)], axis=0).T
        kaug[rows, 0:dh] = k_ref[rows, :]
        kaug[rows, dh:dh + LANES] = f_cols.astype(BF16)
        vt[0:dh, rows] = v_ref[rows, :].T
    pad_row = lax.broadcasted_iota(jnp.int32, (BF16_SUBLANES, 1), 0)
    vt[dh:dh + BF16_SUBLANES, :] = jnp.broadcast_to(
        jnp.where(pad_row == 0, 1.0, 0.0).astype(BF16), (BF16_SUBLANES, seq))

    ones3 = jnp.where(lane < 3, 1.0, 0.0).astype(BF16)
    key_in_chunk = lax.broadcasted_iota(jnp.int32, (tk, tq), 0)
    query_in_block = lax.broadcasted_iota(jnp.int32, (tk, tq), 1)
    blocks = [(qi, j) for qi in range(seq // tq) for j in range((qi + 1) * tq // tk)]

    def logits(qi, j):
        qa = jnp.concatenate([q_ref[qi * tq:(qi + 1) * tq, :],
                              jnp.broadcast_to(ones3, (tq, LANES))], axis=-1)
        return lax.dot_general(kaug[j * tk:(j + 1) * tk, :], qa, (((1,), (1,)), ((), ())),
                               preferred_element_type=F32)

    ahead = [logits(*blk) for blk in blocks[:LOOKAHEAD]]
    for t, (qi, j) in enumerate(blocks):
        s = ahead.pop(0)
        if t + LOOKAHEAD < len(blocks):
            ahead.append(logits(*blocks[t + LOOKAHEAD]))
        q_lo, k_lo = qi * tq, j * tk
        if j == 0:
            m = jnp.full((1, tq), -jnp.inf, F32)
            acc = jnp.zeros((dh + BF16_SUBLANES, tq), F32)
        if k_lo + tk - 1 > q_lo:
            s = jnp.where(key_in_chunk + k_lo <= query_in_block + q_lo, s, -jnp.inf)
        m_new = jnp.maximum(m, jnp.max(s, axis=0, keepdims=True))
        alpha = jnp.exp2(m - m_new)
        p = jnp.exp2(s - m_new)
        acc = alpha * acc + _dot(vt[:, k_lo:k_lo + tk], p.astype(BF16))
        m = m_new
        if k_lo + tk == q_lo + tq:
            inv_l = 1.0 / acc[dh:dh + 1, :]
            o_ref[q_lo:q_lo + tq, :] = (acc[0:dh, :] * inv_l).T.astype(o_ref.dtype)


def _fox_attention(qkv, f_terms, w_up, w_down, *, batch, seq, d, tq=512, tk=512):
    dh = d // N_HEADS
    qkv3 = qkv.reshape(batch, seq, 3 * d)
    head_cols = lambda first: pl.BlockSpec((None, seq, dh), lambda b, h: (b, 0, first + h))
    steps = batch * N_HEADS
    wu2 = w_up.reshape(-1, w_up.shape[-1])
    wd2 = w_down.reshape(-1, w_down.shape[-1])
    slab = lambda w2: pl.BlockSpec((w2.shape[0] // steps, w2.shape[1]),
                                   lambda b, h: (b * N_HEADS + h, 0))
    attn, wu_bf, wd_bf = pl.pallas_call(
        functools.partial(_fox_attn_kernel, tq=tq, tk=tk),
        grid=(batch, N_HEADS),
        in_specs=[
            head_cols(0),
            head_cols(N_HEADS),
            head_cols(2 * N_HEADS),
            pl.BlockSpec((None, 3, N_HEADS, seq), lambda b, h: (b, 0, 0, 0)),
            slab(wu2),
            slab(wd2),
        ],
        out_specs=[head_cols(0), slab(wu2), slab(wd2)],
        out_shape=[jax.ShapeDtypeStruct((batch, seq, d), BF16),
                   jax.ShapeDtypeStruct(wu2.shape, BF16),
                   jax.ShapeDtypeStruct(wd2.shape, BF16)],
        scratch_shapes=[pltpu.VMEM((seq, dh + LANES), BF16),
                        pltpu.VMEM((dh + BF16_SUBLANES, seq), BF16)],
        compiler_params=_params("parallel", "parallel"),
        name="fox_attention",
    )(qkv3, qkv3, qkv3, f_terms, wu2, wd2)
    return attn.reshape(batch * seq, d), wu_bf.reshape(w_up.shape), wd_bf.reshape(w_down.shape)


def _outproj_kernel(a_ref, w_ref, x_ref, g_ref, o_ref):
    o_ref[...] = x_ref[...] + g_ref[...] * _dot(a_ref[...], w_ref[...])


def _outproj(a, w, x, mod, layer, *, seq, tm=512):
    m, d = x.shape
    tpb = seq // tm
    return pl.pallas_call(
        _outproj_kernel,
        grid=(m // tm,),
        in_specs=[
            pl.BlockSpec((tm, d), lambda i: (i, 0)),
            pl.BlockSpec((d, d), lambda i: (0, 0)),
            pl.BlockSpec((tm, d), lambda i: (i, 0)),
            _mod_spec(layer, 2, tpb, d),
        ],
        out_specs=pl.BlockSpec((tm, d), lambda i: (i, 0)),
        out_shape=jax.ShapeDtypeStruct((m, d), F32),
        compiler_params=_params("parallel"),
        name="attn_outproj",
    )(a, w, x, mod)


def _conv_outproj_kernel(b_ref, c_ref, u_ref, ch_ref, uh_ref, cw_ref, w_ref, x_ref, g_ref, o_ref,
                         ext, *, tiles_per_seq):
    tm = c_ref.shape[0]
    halo_rows = ch_ref.shape[0]
    seq_start = pl.program_id(0) % tiles_per_seq == 0
    halo = ch_ref[...].astype(F32) * uh_ref[...].astype(F32)
    ext[0:halo_rows, :] = jnp.where(seq_start, 0.0, halo)
    ext[halo_rows:halo_rows + tm, :] = c_ref[...].astype(F32) * u_ref[...].astype(F32)
    y = None
    for tap in range(CONV_WIDTH):
        start = halo_rows - (CONV_WIDTH - 1 - tap)
        term = cw_ref[tap:tap + 1, :] * ext[start:start + tm, :]
        y = term if y is None else y + term
    z = (b_ref[...].astype(F32) * y).astype(BF16)
    o_ref[...] = x_ref[...] + g_ref[...] * _dot(z, w_ref[...])


def _conv_outproj(proj, conv_w, w, x, mod, layer, *, seq, tm=512):
    m, d = x.shape
    tpb = seq // tm
    halo = BF16_SUBLANES
    halo_idx = lambda col: (lambda i: (jnp.maximum(i * (tm // halo) - 1, 0), col))
    return pl.pallas_call(
        functools.partial(_conv_outproj_kernel, tiles_per_seq=tpb),
        grid=(m // tm,),
        in_specs=[
            pl.BlockSpec((tm, d), lambda i: (i, 0)),
            pl.BlockSpec((tm, d), lambda i: (i, 1)),
            pl.BlockSpec((tm, d), lambda i: (i, 2)),
            pl.BlockSpec((halo, d), halo_idx(1)),
            pl.BlockSpec((halo, d), halo_idx(2)),
            pl.BlockSpec((CONV_WIDTH, d), lambda i: (0, 0)),
            pl.BlockSpec((d, d), lambda i: (0, 0)),
            pl.BlockSpec((tm, d), lambda i: (i, 0)),
            _mod_spec(layer, 2, tpb, d),
        ],
        out_specs=pl.BlockSpec((tm, d), lambda i: (i, 0)),
        out_shape=jax.ShapeDtypeStruct((m, d), F32),
        scratch_shapes=[pltpu.VMEM((halo + tm, d), F32)],
        compiler_params=_params("parallel"),
        name="conv_outproj",
    )(proj, proj, proj, proj, proj, conv_w, w, x, mod)


def _mlp_kernel(x_ref, xn_ref, g_ref, sh_ref, sc_ref, shn_ref, scn_ref, gate_ref, wu_ref, wd_ref,
                fin_ref, o_ref, h_scr, acc, *, final_norm):
    i = pl.program_id(0)
    f = pl.program_id(1)
    slot = i % 2
    tm = x_ref.shape[0]
    rows_per_step = tm // pl.num_programs(1)

    @pl.when(jnp.logical_and(i == 0, f == 0))
    def _():
        h_scr[0] = _norm_modulate(x_ref[...], g_ref[...], sh_ref[...], sc_ref[...]).astype(BF16)

    @pl.when(f == 0)
    def _():
        acc[...] = jnp.zeros_like(acc)

    half = wu_ref.shape[1] // 2
    u_lo = jnp.maximum(_dot(h_scr[slot], wu_ref[:, 0:half]), 0.0)
    rows = pl.ds(pl.multiple_of(f * rows_per_step, rows_per_step), rows_per_step)
    h_scr[1 - slot, rows, :] = _norm_modulate(xn_ref[rows, :], g_ref[...], shn_ref[...],
                                              scn_ref[...]).astype(BF16)
    u_hi = jnp.maximum(_dot(h_scr[slot], wu_ref[:, half:]), 0.0)
    u = jnp.concatenate([u_lo, u_hi], axis=1)
    acc[...] += _dot((u * u).astype(BF16), wd_ref[...])

    @pl.when(f == pl.num_programs(1) - 1)
    def _():
        y = x_ref[...] + gate_ref[...] * acc[...]
        if final_norm:
            inv = lax.rsqrt(jnp.mean(y * y, axis=-1, keepdims=True) + RMS_EPS)
            y = (y * inv) * fin_ref[...]
        o_ref[...] = y


def _mlp(x, gain, mod, layer, w_up, w_down, fin_gain, *, seq, final_norm, tm=512, tf=1024):
    m, d = x.shape
    dff = w_up.shape[2]
    tpb = seq // tm
    last = m // tm - 1
    assert (tm // (dff // tf)) % BF16_SUBLANES == 0
    return pl.pallas_call(
        functools.partial(_mlp_kernel, final_norm=final_norm),
        grid=(m // tm, dff // tf),
        in_specs=[
            pl.BlockSpec((tm, d), lambda i, f: (i, 0)),
            pl.BlockSpec((tm, d), lambda i, f: (jnp.minimum(i + 1, last), 0)),
            pl.BlockSpec((1, d), lambda i, f: (0, 0)),
            _mod_spec(layer, 3, tpb, d),
            _mod_spec(layer, 4, tpb, d),
            _mod_spec(layer, 3, tpb, d, ahead=1, last_tile=last),
            _mod_spec(layer, 4, tpb, d, ahead=1, last_tile=last),
            _mod_spec(layer, 5, tpb, d),
            pl.BlockSpec((None, d, tf), lambda i, f: (layer, 0, f)),
            pl.BlockSpec((None, tf, d), lambda i, f: (layer, f, 0)),
            pl.BlockSpec((1, d), lambda i, f: (0, 0)),
        ],
        out_specs=pl.BlockSpec((tm, d), lambda i, f: (i, 0)),
        out_shape=jax.ShapeDtypeStruct((m, d), F32),
        scratch_shapes=[pltpu.VMEM((2, tm, d), BF16), pltpu.VMEM((tm, d), F32)],
        compiler_params=_params("arbitrary", "arbitrary"),
        name="mlp_final" if final_norm else "mlp",
    )(x, x, gain, mod, mod, mod, mod, mod, w_up, w_down, fin_gain)


def kernel(x, c, ada_w, ada_b, norm_mix, norm_mlp, fox_w_in, fox_b_f, fox_w_out, conv_w_in, conv_w,
           conv_w_out, mlp_w_up, mlp_w_down, final_norm):
    batch, seq, d = x.shape
    depth = ada_w.shape[0]
    dh = d // N_HEADS
    assert depth == 2 and dh == LANES and fox_w_in.shape[-1] == 3 * d + N_HEADS

    c_pad = jnp.pad(c, ((0, BF16_SUBLANES - batch), (0, 0)))
    mod = _ada_modulation(c_pad, ada_w, ada_b)
    mod = mod[:, :batch].reshape(depth, batch, N_MOD, 1, d)

    xf = x.reshape(batch * seq, d)
    row = lambda v: v.reshape(1, d)

    b_f = jnp.pad(fox_b_f[0], (0, LANES - N_HEADS)).reshape(1, LANES)
    qkv, f_logit, w_attn_out, w_conv_in, w_conv_out = _inproj(
        xf, row(norm_mix[0]), mod, 0, fox_w_in[0].astype(BF16), n=3 * d, with_forget=True, seq=seq,
        n_scaled_cols=d, out_scale=LOG2_E * dh ** -0.5,
        casts=(fox_w_out[0], conv_w_in[0], conv_w_out[0]))
    fcum = _forget_cumsum(f_logit, b_f, batch=batch, seq=seq)
    attn, w_up, w_down = _fox_attention(qkv, fcum, mlp_w_up, mlp_w_down, batch=batch, seq=seq, d=d)
    xf = _outproj(attn, w_attn_out, xf, mod, 0, seq=seq)
    xf = _mlp(xf, row(norm_mlp[0]), mod, 0, w_up, w_down, row(final_norm), seq=seq,
              final_norm=False)

    proj = _inproj(xf, row(norm_mix[1]), mod, 1, w_conv_in, n=3 * d, with_forget=False, seq=seq)[0]
    xf = _conv_outproj(proj, conv_w[0], w_conv_out, xf, mod, 1, seq=seq)
    xf = _mlp(xf, row(norm_mlp[1]), mod, 1, w_up, w_down, row(final_norm), seq=seq,
              final_norm=True)
    return xf.reshape(batch, seq, d)
```

```python
import functools

import jax
import jax.numpy as jnp
from jax import lax
from jax.experimental import pallas as pl
from jax.experimental.pallas import tpu as pltpu

N_HEADS = 16
CONV_WIDTH = 3
N_MOD = 6
RMS_EPS = 1e-6
LOG2_E = 1.4426950408889634
LOOKAHEAD = 1

LANES = 128
F32_SUBLANES = 8
BF16_SUBLANES = 16
V7X_VMEM_BYTES = 64 * 1024 * 1024
VMEM_LIMIT_BYTES = V7X_VMEM_BYTES - 8 * 1024 * 1024

F32 = jnp.float32
BF16 = jnp.bfloat16


def _params(*semantics):
    return pltpu.CompilerParams(dimension_semantics=semantics, vmem_limit_bytes=VMEM_LIMIT_BYTES)


def _dot(a, b):
    return jnp.dot(a, b, preferred_element_type=F32)


def _norm_modulate(x, gain, shift, scale):
    inv = lax.rsqrt(jnp.mean(x * x, axis=-1, keepdims=True) + RMS_EPS)
    return (x * inv) * gain * (1.0 + scale) + shift


def _ada_kernel(c_ref, w_ref, b_ref, o_ref):
    c = c_ref[...]
    act = (c * jax.nn.sigmoid(c)).astype(BF16)
    o_ref[...] = _dot(act, w_ref[...].astype(BF16)) + b_ref[...]


def _ada_modulation(c_pad, ada_w, ada_b, tn=1024):
    depth, d, n = ada_w.shape
    rows = c_pad.shape[0]
    return pl.pallas_call(
        _ada_kernel,
        grid=(depth, n // tn),
        in_specs=[
            pl.BlockSpec((rows, d), lambda l, j: (0, 0)),
            pl.BlockSpec((None, d, tn), lambda l, j: (l, 0, j)),
            pl.BlockSpec((None, 1, tn), lambda l, j: (l, 0, j)),
        ],
        out_specs=pl.BlockSpec((None, rows, tn), lambda l, j: (l, 0, j)),
        out_shape=jax.ShapeDtypeStruct((depth, rows, n), F32),
        compiler_params=_params("parallel", "parallel"),
        name="ada_modulation",
    )(c_pad, ada_w, ada_b.reshape(depth, 1, n))


def _mod_spec(layer, which, tiles_per_batch, d, ahead=0, last_tile=None):
    def index(i, *_):
        tile = jnp.minimum(i + ahead, last_tile) if ahead else i
        return (layer, tile // tiles_per_batch, which, 0, 0)
    return pl.BlockSpec((None, None, None, 1, d), index)


def _inproj_kernel(*refs, mixer, out_scale, n_casts, tiles_per_seq):
    x0_ref, xn_ref, g_ref, sh0_ref, sc0_ref, shn_ref, scn_ref, w_ref, side_ref = refs[:9]
    cast_in, refs = refs[9:9 + n_casts], refs[9 + n_casts:]
    o_ref, refs = refs[0], refs[1:]
    if mixer == "fox":
        f_ref, refs = refs[0], refs[1:]
    cast_out, scratch = refs[:n_casts], refs[n_casts:]
    h_scr = scratch[0]
    for src, dst in zip(cast_in, cast_out):
        dst[...] = src[...].astype(dst.dtype)
    i = pl.program_id(0)
    slot = i % 2
    tm, d = xn_ref.shape

    @pl.when(i == 0)
    def _():
        h_scr[0] = _norm_modulate(x0_ref[...], g_ref[...], sh0_ref[...], sc0_ref[...]).astype(BF16)
        if mixer == "conv":
            scratch[2][...] = jnp.zeros_like(scratch[2])

    def column_group(grp):
        return _dot(h_scr[slot], w_ref[:, grp * d:(grp + 1) * d])

    def normalise_next(part, parts):
        rows = slice(part * (tm // parts), (part + 1) * (tm // parts))
        h_scr[1 - slot, rows, :] = _norm_modulate(xn_ref[rows, :], g_ref[...], shn_ref[...],
                                                  scn_ref[...]).astype(BF16)

    if mixer == "fox":
        o_ref[:, 0:d] = (column_group(0) * out_scale).astype(o_ref.dtype)
        normalise_next(0, 2)
        o_ref[:, d:2 * d] = column_group(1).astype(o_ref.dtype)
        normalise_next(1, 2)
        o_ref[:, 2 * d:3 * d] = column_group(2).astype(o_ref.dtype)
        lane = lax.broadcasted_iota(jnp.int32, (1, LANES), 1)
        f_ref[...] = _dot(h_scr[slot], jnp.where(lane < N_HEADS, side_ref[...], 0.0).astype(BF16))
    else:
        ext, carry = scratch[1], scratch[2]
        c_gate = column_group(1)
        normalise_next(0, 2)
        gated = c_gate * column_group(2)
        normalise_next(1, 2)
        halo = carry.shape[0]
        ext[0:halo, :] = jnp.where(i % tiles_per_seq == 0, 0.0, carry[...])
        ext[halo:halo + tm, :] = gated
        carry[...] = gated[tm - halo:tm, :]
        y = None
        for tap in range(CONV_WIDTH):
            start = halo - (CONV_WIDTH - 1 - tap)
            term = side_ref[tap:tap + 1, :] * ext[start:start + tm, :]
            y = term if y is None else y + term
        o_ref[...] = (column_group(0) * y).astype(o_ref.dtype)


def _inproj(x, gain, mod, layer, w, *, mixer, seq, side, out_scale=1.0, tm=256, casts=()):
    m, d = x.shape
    n = 3 * d
    tpb = seq // tm
    last = m // tm - 1
    steps = m // tm
    slab = lambda a: pl.BlockSpec((a.shape[0] // steps, a.shape[1]), lambda i: (i, 0))
    in_specs = [
        pl.BlockSpec((tm, d), lambda i: (0, 0)),
        pl.BlockSpec((tm, d), lambda i: (jnp.minimum(i + 1, last), 0)),
        pl.BlockSpec((1, d), lambda i: (0, 0)),
        _mod_spec(layer, 0, tpb, d),
        _mod_spec(layer, 1, tpb, d),
        _mod_spec(layer, 0, tpb, d, ahead=1, last_tile=last),
        _mod_spec(layer, 1, tpb, d, ahead=1, last_tile=last),
        pl.BlockSpec((d, n), lambda i: (0, 0), pipeline_mode=pl.Buffered(1)),
    ]
    args = [x, x, gain, mod, mod, mod, mod, w, side]
    scratch = [pltpu.VMEM((2, tm, d), BF16)]
    if mixer == "fox":
        in_specs.append(pl.BlockSpec((d, LANES), lambda i: (0, n // LANES)))
        out_specs = [pl.BlockSpec((tm, n), lambda i: (i, 0)),
                     pl.BlockSpec((tm, LANES), lambda i: (i, 0))]
        out_shape = [jax.ShapeDtypeStruct((m, n), BF16), jax.ShapeDtypeStruct((m, LANES), F32)]
    else:
        in_specs.append(pl.BlockSpec(side.shape, lambda i: (0, 0)))
        out_specs = [pl.BlockSpec((tm, d), lambda i: (i, 0))]
        out_shape = [jax.ShapeDtypeStruct((m, d), BF16)]
        scratch += [pltpu.VMEM((F32_SUBLANES + tm, d), F32), pltpu.VMEM((F32_SUBLANES, d), F32)]
    for a in casts:
        assert (a.shape[0] // steps) % BF16_SUBLANES == 0
        in_specs.append(slab(a))
        args.append(a)
        out_specs.append(slab(a))
        out_shape.append(jax.ShapeDtypeStruct(a.shape, BF16))
    kern = functools.partial(_inproj_kernel, mixer=mixer, out_scale=out_scale, n_casts=len(casts),
                             tiles_per_seq=tpb)
    return pl.pallas_call(
        kern,
        grid=(m // tm,),
        in_specs=in_specs,
        out_specs=out_specs,
        out_shape=out_shape,
        scratch_shapes=scratch,
        compiler_params=_params("arbitrary"),
        name="inproj_" + mixer,
    )(*args)


def _split3(x):
    hi = x.astype(BF16).astype(F32)
    r = x - hi
    mid = r.astype(BF16).astype(F32)
    lo = (r - mid).astype(BF16).astype(F32)
    return hi, mid, lo


def _forget_cumsum_kernel(fl_ref, bf_ref, o_ref, carry):
    @pl.when(pl.program_id(1) == 0)
    def _():
        carry[...] = jnp.zeros_like(carry)

    z = fl_ref[...] + bf_ref[...]
    log_f = jnp.minimum(z, 0.0) - jnp.log1p(jnp.exp(-jnp.abs(z)))
    tc = z.shape[0]
    row = lax.broadcasted_iota(jnp.int32, (tc, tc), 0)
    col = lax.broadcasted_iota(jnp.int32, (tc, tc), 1)
    tri = jnp.where(col <= row, 1.0, 0.0).astype(BF16)
    hi, mid, lo = _split3(log_f)
    csum = (_dot(tri, hi.astype(BF16)) + _dot(tri, mid.astype(BF16))) + _dot(tri, lo.astype(BF16))
    out = csum + carry[...]
    carry[...] = out[tc - 1:tc, :]
    neg_f = (out * -LOG2_E).T[0:N_HEADS, :]
    for plane, term in enumerate(_split3(neg_f)):
        o_ref[plane] = term


def _forget_cumsum(f_logit, b_f_pad, *, batch, seq, tc=512):
    fl = f_logit.reshape(batch, seq, LANES)
    return pl.pallas_call(
        _forget_cumsum_kernel,
        grid=(batch, seq // tc),
        in_specs=[
            pl.BlockSpec((None, tc, LANES), lambda b, i: (b, i, 0)),
            pl.BlockSpec((1, LANES), lambda b, i: (0, 0)),
        ],
        out_specs=pl.BlockSpec((None, 3, N_HEADS, tc), lambda b, i: (b, 0, 0, i)),
        out_shape=jax.ShapeDtypeStruct((batch, 3, N_HEADS, seq), F32),
        scratch_shapes=[pltpu.VMEM((1, LANES), F32)],
        compiler_params=_params("parallel", "arbitrary"),
        name="forget_cumsum",
    )(fl, b_f_pad)


def _fox_attn_kernel(q_ref, k_ref, v_ref, f_ref, wu_ref, wd_ref, o_ref, wu_bf_ref, wd_bf_ref,
                     kaug, vt, *, tq, tk):
    wu_bf_ref[...] = wu_ref[...].astype(BF16)
    wd_bf_ref[...] = wd_ref[...].astype(BF16)
    seq, dh = k_ref.shape
    head = pl.ds(pl.program_id(1), 1)
    sub = lax.broadcasted_iota(jnp.int32, (F32_SUBLANES, 1), 0)
    lane = lax.broadcasted_iota(jnp.int32, (1, LANES), 1)

    for c in range(seq // tk):
        rows = slice(c * tk, (c + 1) * tk)
        f_rows = jnp.where(sub == 0, f_ref[0, head, rows], jnp.where(
            sub == 1, f_ref[1, head, rows], jnp.where(sub == 2, f_ref[2, head, rows], 0.0)))
        f_cols = jnp.concatenate([f_rows, jnp.zeros((LANES - F32_SUBLANES, tk), F32)],
                                 axis=0).T
        kaug[rows, 0:dh] = k_ref[rows, :]
        kaug[rows, dh:dh + LANES] = f_cols.astype(BF16)
        vt[0:dh, rows] = v_ref[rows, :].T
    pad_row = lax.broadcasted_iota(jnp.int32, (BF16_SUBLANES, 1), 0)
    vt[dh:dh + BF16_SUBLANES, :] = jnp.broadcast_to(
        jnp.where(pad_row == 0, 1.0, 0.0).astype(BF16), (BF16_SUBLANES, seq))

    ones3 = jnp.where(lane < 3, 1.0, 0.0).astype(BF16)
    key_in_chunk = lax.broadcasted_iota(jnp.int32, (tk, tq), 0)
    query_in_block = lax.broadcasted_iota(jnp.int32, (tk, tq), 1)
    blocks = [(qi, j) for qi in range(seq // tq) for j in range((qi + 1) * tq // tk)]

    def logits(qi, j):
        qa = jnp.concatenate([q_ref[qi * tq:(qi + 1) * tq, :],
                              jnp.broadcast_to(ones3, (tq, LANES))], axis=-1)
        return lax.dot_general(kaug[j * tk:(j + 1) * tk, :], qa, (((1,), (1,)), ((), ())),
                               preferred_element_type=F32)

    ahead = [logits(*blk) for blk in blocks[:LOOKAHEAD]]
    for t, (qi, j) in enumerate(blocks):
        s = ahead.pop(0)
        if t + LOOKAHEAD < len(blocks):
            ahead.append(logits(*blocks[t + LOOKAHEAD]))
        q_lo, k_lo = qi * tq, j * tk
        if j == 0:
            m = jnp.full((1, tq), -jnp.inf, F32)
            acc = jnp.zeros((dh + BF16_SUBLANES, tq), F32)
        if k_lo + tk - 1 > q_lo:
            s = jnp.where(key_in_chunk + k_lo <= query_in_block + q_lo, s, -jnp.inf)
        m_new = jnp.maximum(m, jnp.max(s, axis=0, keepdims=True))
        alpha = jnp.exp2(m - m_new)
        p = jnp.exp2(s - m_new)
        acc = alpha * acc + _dot(vt[:, k_lo:k_lo + tk], p.astype(BF16))
        m = m_new
        if k_lo + tk == q_lo + tq:
            inv_l = 1.0 / acc[dh:dh + 1, :]
            o_ref[q_lo:q_lo + tq, :] = (acc[0:dh, :] * inv_l).T.astype(o_ref.dtype)


def _fox_attention(qkv, f_terms, w_up, w_down, *, batch, seq, d, tq=512, tk=512):
    dh = d // N_HEADS
    qkv3 = qkv.reshape(batch, seq, 3 * d)
    head_cols = lambda first: pl.BlockSpec((None, seq, dh), lambda b, h: (b, 0, first + h))
    steps = batch * N_HEADS
    wu2 = w_up.reshape(-1, w_up.shape[-1])
    wd2 = w_down.reshape(-1, w_down.shape[-1])
    slab = lambda w2: pl.BlockSpec((w2.shape[0] // steps, w2.shape[1]),
                                   lambda b, h: (b * N_HEADS + h, 0))
    attn, wu_bf, wd_bf = pl.pallas_call(
        functools.partial(_fox_attn_kernel, tq=tq, tk=tk),
        grid=(batch, N_HEADS),
        in_specs=[
            head_cols(0),
            head_cols(N_HEADS),
            head_cols(2 * N_HEADS),
            pl.BlockSpec((None, 3, N_HEADS, seq), lambda b, h: (b, 0, 0, 0)),
            slab(wu2),
            slab(wd2),
        ],
        out_specs=[head_cols(0), slab(wu2), slab(wd2)],
        out_shape=[jax.ShapeDtypeStruct((batch, seq, d), BF16),
                   jax.ShapeDtypeStruct(wu2.shape, BF16),
                   jax.ShapeDtypeStruct(wd2.shape, BF16)],
        scratch_shapes=[pltpu.VMEM((seq, dh + LANES), BF16),
                        pltpu.VMEM((dh + BF16_SUBLANES, seq), BF16)],
        compiler_params=_params("parallel", "parallel"),
        name="fox_attention",
    )(qkv3, qkv3, qkv3, f_terms, wu2, wd2)
    return attn.reshape(batch * seq, d), wu_bf.reshape(w_up.shape), wd_bf.reshape(w_down.shape)


def _outproj_kernel(a_ref, w_ref, x_ref, g_ref, o_ref):
    o_ref[...] = x_ref[...] + g_ref[...] * _dot(a_ref[...], w_ref[...])


def _outproj(a, w, x, mod, layer, *, seq, tm=512):
    m, d = x.shape
    tpb = seq // tm
    return pl.pallas_call(
        _outproj_kernel,
        grid=(m // tm,),
        in_specs=[
            pl.BlockSpec((tm, d), lambda i: (i, 0)),
            pl.BlockSpec((d, d), lambda i: (0, 0)),
            pl.BlockSpec((tm, d), lambda i: (i, 0)),
            _mod_spec(layer, 2, tpb, d),
        ],
        out_specs=pl.BlockSpec((tm, d), lambda i: (i, 0)),
        out_shape=jax.ShapeDtypeStruct((m, d), F32),
        compiler_params=_params("parallel"),
        name="mixer_outproj",
    )(a, w, x, mod)


def _mlp_kernel(x_ref, xn_ref, g_ref, sh_ref, sc_ref, shn_ref, scn_ref, gate_ref, wu_ref, wd_ref,
                fin_ref, o_ref, h_scr, acc, *, final_norm):
    i = pl.program_id(0)
    f = pl.program_id(1)
    slot = i % 2
    tm = x_ref.shape[0]
    rows_per_step = tm // pl.num_programs(1)

    @pl.when(jnp.logical_and(i == 0, f == 0))
    def _():
        h_scr[0] = _norm_modulate(x_ref[...], g_ref[...], sh_ref[...], sc_ref[...]).astype(BF16)
        acc[...] = jnp.zeros_like(acc)

    half = wu_ref.shape[1] // 2
    u_lo = jnp.maximum(_dot(h_scr[slot], wu_ref[:, 0:half]), 0.0)
    rows = pl.ds(pl.multiple_of(f * rows_per_step, rows_per_step), rows_per_step)
    h_scr[1 - slot, rows, :] = _norm_modulate(xn_ref[rows, :], g_ref[...], shn_ref[...],
                                              scn_ref[...]).astype(BF16)
    u_hi = jnp.maximum(_dot(h_scr[slot], wu_ref[:, half:]), 0.0)
    u = jnp.concatenate([u_lo, u_hi], axis=1)
    acc[...] = _dot((u * u).astype(BF16), wd_ref[...]) + jnp.where(f == 0, 0.0, acc[...])

    @pl.when(f == pl.num_programs(1) - 1)
    def _():
        y = x_ref[...] + gate_ref[...] * acc[...]
        if final_norm:
            inv = lax.rsqrt(jnp.mean(y * y, axis=-1, keepdims=True) + RMS_EPS)
            y = (y * inv) * fin_ref[...]
        o_ref[...] = y


def _mlp(x, gain, mod, layer, w_up, w_down, fin_gain, *, seq, final_norm, tm=512, tf=1024):
    m, d = x.shape
    dff = w_up.shape[2]
    tpb = seq // tm
    last = m // tm - 1
    assert (tm // (dff // tf)) % BF16_SUBLANES == 0
    return pl.pallas_call(
        functools.partial(_mlp_kernel, final_norm=final_norm),
        grid=(m // tm, dff // tf),
        in_specs=[
            pl.BlockSpec((tm, d), lambda i, f: (i, 0)),
            pl.BlockSpec((tm, d), lambda i, f: (jnp.minimum(i + 1, last), 0)),
            pl.BlockSpec((1, d), lambda i, f: (0, 0)),
            _mod_spec(layer, 3, tpb, d),
            _mod_spec(layer, 4, tpb, d),
            _mod_spec(layer, 3, tpb, d, ahead=1, last_tile=last),
            _mod_spec(layer, 4, tpb, d, ahead=1, last_tile=last),
            _mod_spec(layer, 5, tpb, d),
            pl.BlockSpec((None, d, tf), lambda i, f: (layer, 0, f)),
            pl.BlockSpec((None, tf, d), lambda i, f: (layer, f, 0)),
            pl.BlockSpec((1, d), lambda i, f: (0, 0)),
        ],
        out_specs=pl.BlockSpec((tm, d), lambda i, f: (i, 0)),
        out_shape=jax.ShapeDtypeStruct((m, d), F32),
        scratch_shapes=[pltpu.VMEM((2, tm, d), BF16), pltpu.VMEM((tm, d), F32)],
        compiler_params=_params("arbitrary", "arbitrary"),
        name="mlp_final" if final_norm else "mlp",
    )(x, x, gain, mod, mod, mod, mod, mod, w_up, w_down, fin_gain)


def kernel(x, c, ada_w, ada_b, norm_mix, norm_mlp, fox_w_in, fox_b_f, fox_w_out, conv_w_in, conv_w,
           conv_w_out, mlp_w_up, mlp_w_down, final_norm):
    batch, seq, d = x.shape
    depth = ada_w.shape[0]
    dh = d // N_HEADS
    assert depth == 2 and dh == LANES and fox_w_in.shape[-1] == 3 * d + N_HEADS

    c_pad = jnp.pad(c, ((0, BF16_SUBLANES - batch), (0, 0)))
    mod = _ada_modulation(c_pad, ada_w, ada_b)
    mod = mod[:, :batch].reshape(depth, batch, N_MOD, 1, d)

    xf = x.reshape(batch * seq, d)
    row = lambda v: v.reshape(1, d)

    b_f = jnp.pad(fox_b_f[0], (0, LANES - N_HEADS)).reshape(1, LANES)
    w_fox_in = fox_w_in[0].astype(BF16)
    qkv, f_logit, w_attn_out, w_conv_in, w_conv_out = _inproj(
        xf, row(norm_mix[0]), mod, 0, w_fox_in, mixer="fox", side=w_fox_in, seq=seq,
        out_scale=LOG2_E * dh ** -0.5, casts=(fox_w_out[0], conv_w_in[0], conv_w_out[0]))
    fcum = _forget_cumsum(f_logit, b_f, batch=batch, seq=seq)
    attn, w_up, w_down = _fox_attention(qkv, fcum, mlp_w_up, mlp_w_down, batch=batch, seq=seq, d=d)
    xf = _outproj(attn, w_attn_out, xf, mod, 0, seq=seq)
    xf = _mlp(xf, row(norm_mlp[0]), mod, 0, w_up, w_down, row(final_norm), seq=seq,
              final_norm=False)

    gated = _inproj(xf, row(norm_mix[1]), mod, 1, w_conv_in, mixer="conv", side=conv_w[0],
                    seq=seq)[0]
    xf = _outproj(gated, w_conv_out, xf, mod, 1, seq=seq)
    xf = _mlp(xf, row(norm_mlp[1]), mod, 1, w_up, w_down, row(final_norm), seq=seq,
              final_norm=True)
    return xf.reshape(batch, seq, d)
```

```python
import functools

import jax
import jax.numpy as jnp
from jax import lax
from jax.experimental import pallas as pl
from jax.experimental.pallas import tpu as pltpu

N_HEADS = 16
CONV_WIDTH = 3
N_MOD = 6
RMS_EPS = 1e-6
LOG2_E = 1.4426950408889634

LANES = 128
F32_SUBLANES = 8
BF16_SUBLANES = 16
V7X_VMEM_BYTES = 64 * 1024 * 1024
VMEM_LIMIT_BYTES = V7X_VMEM_BYTES - 8 * 1024 * 1024

F32 = jnp.float32
BF16 = jnp.bfloat16


def _params(*semantics):
    return pltpu.CompilerParams(dimension_semantics=semantics, vmem_limit_bytes=VMEM_LIMIT_BYTES)


def _dot(a, b):
    return jnp.dot(a, b, preferred_element_type=F32)


def _norm_modulate(x, gain, shift, scale):
    inv = lax.rsqrt(jnp.mean(x * x, axis=-1, keepdims=True) + RMS_EPS)
    return (x * inv) * gain * (1.0 + scale) + shift


def _ada_kernel(c_ref, w_ref, b_ref, o_ref):
    c = c_ref[...]
    act = (c * jax.nn.sigmoid(c)).astype(BF16)
    o_ref[...] = _dot(act, w_ref[...].astype(BF16)) + b_ref[...]


def _ada_modulation(c_pad, ada_w, ada_b, tn=1024):
    depth, d, n = ada_w.shape
    rows = c_pad.shape[0]
    return pl.pallas_call(
        _ada_kernel,
        grid=(depth, n // tn),
        in_specs=[
            pl.BlockSpec((rows, d), lambda l, j: (0, 0)),
            pl.BlockSpec((None, d, tn), lambda l, j: (l, 0, j)),
            pl.BlockSpec((None, 1, tn), lambda l, j: (l, 0, j)),
        ],
        out_specs=pl.BlockSpec((None, rows, tn), lambda l, j: (l, 0, j)),
        out_shape=jax.ShapeDtypeStruct((depth, rows, n), F32),
        compiler_params=_params("parallel", "parallel"),
        name="ada_modulation",
    )(c_pad, ada_w, ada_b.reshape(depth, 1, n))


def _mod_spec(layer, which, tiles_per_batch, d, ahead=0, last_tile=None):
    def index(i, *_):
        tile = jnp.minimum(i + ahead, last_tile) if ahead else i
        return (layer, tile // tiles_per_batch, which, 0, 0)
    return pl.BlockSpec((None, None, None, 1, d), index)


def _inproj_kernel(*refs, mixer, out_scale, n_casts, tiles_per_seq):
    x0_ref, xn_ref, g_ref, sh0_ref, sc0_ref, shn_ref, scn_ref, w_ref, side_ref = refs[:9]
    cast_in, refs = refs[9:9 + n_casts], refs[9 + n_casts:]
    o_ref, refs = refs[0], refs[1:]
    if mixer == "fox":
        f_ref, refs = refs[0], refs[1:]
    cast_out, scratch = refs[:n_casts], refs[n_casts:]
    h_scr = scratch[0]
    for src, dst in zip(cast_in, cast_out):
        dst[...] = src[...].astype(dst.dtype)
    i = pl.program_id(0)
    slot = i % 2
    tm, d = xn_ref.shape

    @pl.when(i == 0)
    def _():
        h_scr[0] = _norm_modulate(x0_ref[...], g_ref[...], sh0_ref[...], sc0_ref[...]).astype(BF16)
        if mixer == "conv":
            scratch[2][...] = jnp.zeros_like(scratch[2])

    def column_group(grp):
        return _dot(h_scr[slot], w_ref[:, grp * d:(grp + 1) * d])

    def normalise_next(part, parts):
        rows = slice(part * (tm // parts), (part + 1) * (tm // parts))
        h_scr[1 - slot, rows, :] = _norm_modulate(xn_ref[rows, :], g_ref[...], shn_ref[...],
                                                  scn_ref[...]).astype(BF16)

    if mixer == "fox":
        o_ref[:, 0:d] = (column_group(0) * out_scale).astype(o_ref.dtype)
        normalise_next(0, 2)
        o_ref[:, d:2 * d] = column_group(1).astype(o_ref.dtype)
        normalise_next(1, 2)
        o_ref[:, 2 * d:3 * d] = column_group(2).astype(o_ref.dtype)
        lane = lax.broadcasted_iota(jnp.int32, (1, LANES), 1)
        f_ref[...] = _dot(h_scr[slot], jnp.where(lane < N_HEADS, side_ref[...], 0.0).astype(BF16))
    else:
        ext, carry = scratch[1], scratch[2]
        c_gate = column_group(1)
        normalise_next(0, 2)
        gated = c_gate * column_group(2)
        normalise_next(1, 2)
        halo = carry.shape[0]
        ext[0:halo, :] = jnp.where(i % tiles_per_seq == 0, 0.0, carry[...])
        ext[halo:halo + tm, :] = gated
        carry[...] = gated[tm - halo:tm, :]
        y = None
        for tap in range(CONV_WIDTH):
            start = halo - (CONV_WIDTH - 1 - tap)
            term = side_ref[tap:tap + 1, :] * ext[start:start + tm, :]
            y = term if y is None else y + term
        o_ref[...] = (column_group(0) * y).astype(o_ref.dtype)


def _inproj(x, gain, mod, layer, w, *, mixer, seq, side, out_scale=1.0, tm=256, casts=()):
    m, d = x.shape
    n = 3 * d
    tpb = seq // tm
    last = m // tm - 1
    steps = m // tm
    slab = lambda a: pl.BlockSpec((a.shape[0] // steps, a.shape[1]), lambda i: (i, 0))
    in_specs = [
        pl.BlockSpec((tm, d), lambda i: (0, 0)),
        pl.BlockSpec((tm, d), lambda i: (jnp.minimum(i + 1, last), 0)),
        pl.BlockSpec((1, d), lambda i: (0, 0)),
        _mod_spec(layer, 0, tpb, d),
        _mod_spec(layer, 1, tpb, d),
        _mod_spec(layer, 0, tpb, d, ahead=1, last_tile=last),
        _mod_spec(layer, 1, tpb, d, ahead=1, last_tile=last),
        pl.BlockSpec((d, n), lambda i: (0, 0), pipeline_mode=pl.Buffered(1)),
    ]
    args = [x, x, gain, mod, mod, mod, mod, w, side]
    scratch = [pltpu.VMEM((2, tm, d), BF16)]
    if mixer == "fox":
        in_specs.append(pl.BlockSpec((d, LANES), lambda i: (0, n // LANES)))
        out_specs = [pl.BlockSpec((tm, n), lambda i: (i, 0)),
                     pl.BlockSpec((tm, LANES), lambda i: (i, 0))]
        out_shape = [jax.ShapeDtypeStruct((m, n), BF16), jax.ShapeDtypeStruct((m, LANES), F32)]
    else:
        in_specs.append(pl.BlockSpec(side.shape, lambda i: (0, 0)))
        out_specs = [pl.BlockSpec((tm, d), lambda i: (i, 0))]
        out_shape = [jax.ShapeDtypeStruct((m, d), BF16)]
        scratch += [pltpu.VMEM((F32_SUBLANES + tm, d), F32), pltpu.VMEM((F32_SUBLANES, d), F32)]
    for a in casts:
        assert (a.shape[0] // steps) % BF16_SUBLANES == 0
        in_specs.append(slab(a))
        args.append(a)
        out_specs.append(slab(a))
        out_shape.append(jax.ShapeDtypeStruct(a.shape, BF16))
    kern = functools.partial(_inproj_kernel, mixer=mixer, out_scale=out_scale, n_casts=len(casts),
                             tiles_per_seq=tpb)
    return pl.pallas_call(
        kern,
        grid=(m // tm,),
        in_specs=in_specs,
        out_specs=out_specs,
        out_shape=out_shape,
        scratch_shapes=scratch,
        compiler_params=_params("arbitrary"),
        name="inproj_" + mixer,
    )(*args)


def _split3(x):
    hi = x.astype(BF16).astype(F32)
    r = x - hi
    mid = r.astype(BF16).astype(F32)
    lo = (r - mid).astype(BF16).astype(F32)
    return hi, mid, lo


def _forget_cumsum_kernel(fl_ref, bf_ref, o_ref, carry):
    @pl.when(pl.program_id(1) == 0)
    def _():
        carry[...] = jnp.zeros_like(carry)

    z = fl_ref[...] + bf_ref[...]
    log_f = jnp.minimum(z, 0.0) - jnp.log1p(jnp.exp(-jnp.abs(z)))
    tc = z.shape[0]
    row = lax.broadcasted_iota(jnp.int32, (tc, tc), 0)
    col = lax.broadcasted_iota(jnp.int32, (tc, tc), 1)
    tri = jnp.where(col <= row, 1.0, 0.0).astype(BF16)
    hi, mid, lo = _split3(log_f)
    csum = (_dot(tri, hi.astype(BF16)) + _dot(tri, mid.astype(BF16))) + _dot(tri, lo.astype(BF16))
    out = csum + carry[...]
    carry[...] = out[tc - 1:tc, :]
    neg_f = (out * -LOG2_E).T[0:N_HEADS, :]
    for plane, term in enumerate(_split3(neg_f)):
        o_ref[plane] = term


def _forget_cumsum(f_logit, b_f_pad, *, batch, seq, tc=512):
    fl = f_logit.reshape(batch, seq, LANES)
    return pl.pallas_call(
        _forget_cumsum_kernel,
        grid=(batch, seq // tc),
        in_specs=[
            pl.BlockSpec((None, tc, LANES), lambda b, i: (b, i, 0)),
            pl.BlockSpec((1, LANES), lambda b, i: (0, 0)),
        ],
        out_specs=pl.BlockSpec((None, 3, N_HEADS, tc), lambda b, i: (b, 0, 0, i)),
        out_shape=jax.ShapeDtypeStruct((batch, 3, N_HEADS, seq), F32),
        scratch_shapes=[pltpu.VMEM((1, LANES), F32)],
        compiler_params=_params("parallel", "arbitrary"),
        name="forget_cumsum",
    )(fl, b_f_pad)


def _fox_attn_kernel(q_ref, k_ref, v_ref, f_ref, wu_ref, wd_ref, o_ref, wu_bf_ref, wd_bf_ref,
                     kaug, vt, qt, *, tq, tk):
    wu_bf_ref[...] = wu_ref[...].astype(BF16)
    wd_bf_ref[...] = wd_ref[...].astype(BF16)
    seq, dh = k_ref.shape
    head = pl.ds(pl.program_id(1), 1)
    sub = lax.broadcasted_iota(jnp.int32, (F32_SUBLANES, 1), 0)

    for c in range(seq // tk):
        rows = slice(c * tk, (c + 1) * tk)
        f_rows = jnp.where(sub == 0, f_ref[0, head, rows], jnp.where(
            sub == 1, f_ref[1, head, rows], jnp.where(sub == 2, f_ref[2, head, rows], 0.0)))
        f_cols = jnp.concatenate([f_rows, jnp.zeros((LANES - F32_SUBLANES, tk), F32)],
                                 axis=0).T
        kaug[rows, 0:dh] = k_ref[rows, :]
        kaug[rows, dh:dh + LANES] = f_cols.astype(BF16)
        vt[0:dh, rows] = v_ref[rows, :].T
        qt[0:dh, rows] = q_ref[rows, :].T
    pad_row = lax.broadcasted_iota(jnp.int32, (BF16_SUBLANES, 1), 0)
    vt[dh:dh + BF16_SUBLANES, :] = jnp.broadcast_to(
        jnp.where(pad_row == 0, 1.0, 0.0).astype(BF16), (BF16_SUBLANES, seq))
    aug_row = lax.broadcasted_iota(jnp.int32, (LANES, 1), 0)
    qt[dh:dh + LANES, :] = jnp.broadcast_to(
        jnp.where(aug_row < 3, 1.0, 0.0).astype(BF16), (LANES, seq))

    key_in_chunk = lax.broadcasted_iota(jnp.int32, (tk, tq), 0)
    query_in_block = lax.broadcasted_iota(jnp.int32, (tk, tq), 1)
    blocks = [(qi, j) for qi in range(seq // tq) for j in range((qi + 1) * tq // tk)]

    def logits(qi, j):
        return _dot(kaug[j * tk:(j + 1) * tk, :], qt[:, qi * tq:(qi + 1) * tq])

    s_next = logits(*blocks[0])
    for t, (qi, j) in enumerate(blocks):
        s = s_next
        if t + 1 < len(blocks):
            s_next = logits(*blocks[t + 1])
        q_lo, k_lo = qi * tq, j * tk
        if j == 0:
            m = jnp.full((1, tq), -jnp.inf, F32)
            acc = jnp.zeros((dh + BF16_SUBLANES, tq), F32)
        if k_lo + tk - 1 > q_lo:
            s = jnp.where(key_in_chunk + k_lo <= query_in_block + q_lo, s, -jnp.inf)
        m_new = jnp.maximum(m, jnp.max(s, axis=0, keepdims=True))
        alpha = jnp.exp2(m - m_new)
        p = jnp.exp2(s - m_new)
        acc = alpha * acc + _dot(vt[:, k_lo:k_lo + tk], p.astype(BF16))
        m = m_new
        if k_lo + tk == q_lo + tq:
            inv_l = 1.0 / acc[dh:dh + 1, :]
            o_ref[q_lo:q_lo + tq, :] = (acc[0:dh, :] * inv_l).T.astype(o_ref.dtype)


def _fox_attention(qkv, f_terms, w_up, w_down, *, batch, seq, d, tq=512, tk=512):
    dh = d // N_HEADS
    qkv3 = qkv.reshape(batch, seq, 3 * d)
    head_cols = lambda part: pl.BlockSpec((None, seq, dh), lambda b, h: (b, 0, part * N_HEADS + h))
    steps = batch * N_HEADS
    wu2 = w_up.reshape(-1, w_up.shape[-1])
    wd2 = w_down.reshape(-1, w_down.shape[-1])
    slab = lambda w2: pl.BlockSpec((w2.shape[0] // steps, w2.shape[1]),
                                   lambda b, h: (b * N_HEADS + h, 0))
    attn, wu_bf, wd_bf = pl.pallas_call(
        functools.partial(_fox_attn_kernel, tq=tq, tk=tk),
        grid=(batch, N_HEADS),
        in_specs=[
            head_cols(0),
            head_cols(1),
            head_cols(2),
            pl.BlockSpec((None, 3, N_HEADS, seq), lambda b, h: (b, 0, 0, 0)),
            slab(wu2),
            slab(wd2),
        ],
        out_specs=[head_cols(0), slab(wu2), slab(wd2)],
        out_shape=[jax.ShapeDtypeStruct((batch, seq, d), BF16),
                   jax.ShapeDtypeStruct(wu2.shape, BF16),
                   jax.ShapeDtypeStruct(wd2.shape, BF16)],
        scratch_shapes=[pltpu.VMEM((seq, dh + LANES), BF16),
                        pltpu.VMEM((dh + BF16_SUBLANES, seq), BF16),
                        pltpu.VMEM((dh + LANES, seq), BF16)],
        compiler_params=_params("parallel", "parallel"),
        name="fox_attention",
    )(qkv3, qkv3, qkv3, f_terms, wu2, wd2)
    return attn.reshape(batch * seq, d), wu_bf.reshape(w_up.shape), wd_bf.reshape(w_down.shape)


def _outproj_kernel(a_ref, w_ref, x_ref, g_ref, o_ref):
    o_ref[...] = x_ref[...] + g_ref[...] * _dot(a_ref[...], w_ref[...])


def _outproj(a, w, x, mod, layer, *, seq, tm=512):
    m, d = x.shape
    tpb = seq // tm
    return pl.pallas_call(
        _outproj_kernel,
        grid=(m // tm,),
        in_specs=[
            pl.BlockSpec((tm, d), lambda i: (i, 0)),
            pl.BlockSpec((d, d), lambda i: (0, 0)),
            pl.BlockSpec((tm, d), lambda i: (i, 0)),
            _mod_spec(layer, 2, tpb, d),
        ],
        out_specs=pl.BlockSpec((tm, d), lambda i: (i, 0)),
        out_shape=jax.ShapeDtypeStruct((m, d), F32),
        compiler_params=_params("parallel"),
        name="mixer_outproj",
    )(a, w, x, mod)


def _mlp_kernel(x_ref, xn_ref, g_ref, sh_ref, sc_ref, shn_ref, scn_ref, gate_ref, wu_ref, wd_ref,
                fin_ref, o_ref, h_scr, acc, *, final_norm):
    i = pl.program_id(0)
    f = pl.program_id(1)
    slot = i % 2
    tm = x_ref.shape[0]
    rows_per_step = tm // pl.num_programs(1)

    @pl.when(jnp.logical_and(i == 0, f == 0))
    def _():
        h_scr[0] = _norm_modulate(x_ref[...], g_ref[...], sh_ref[...], sc_ref[...]).astype(BF16)
        acc[...] = jnp.zeros_like(acc)

    half = wu_ref.shape[1] // 2
    u_lo = jnp.maximum(_dot(h_scr[slot], wu_ref[:, 0:half]), 0.0)
    rows = pl.ds(pl.multiple_of(f * rows_per_step, rows_per_step), rows_per_step)
    h_scr[1 - slot, rows, :] = _norm_modulate(xn_ref[rows, :], g_ref[...], shn_ref[...],
                                              scn_ref[...]).astype(BF16)
    u_hi = jnp.maximum(_dot(h_scr[slot], wu_ref[:, half:]), 0.0)
    u = jnp.concatenate([u_lo, u_hi], axis=1)
    acc[...] = _dot((u * u).astype(BF16), wd_ref[...]) + jnp.where(f == 0, 0.0, acc[...])

    @pl.when(f == pl.num_programs(1) - 1)
    def _():
        y = x_ref[...] + gate_ref[...] * acc[...]
        if final_norm:
            inv = lax.rsqrt(jnp.mean(y * y, axis=-1, keepdims=True) + RMS_EPS)
            y = (y * inv) * fin_ref[...]
        o_ref[...] = y


def _mlp(x, gain, mod, layer, w_up, w_down, fin_gain, *, seq, final_norm, tm=512, tf=1024):
    m, d = x.shape
    dff = w_up.shape[2]
    tpb = seq // tm
    last = m // tm - 1
    assert (tm // (dff // tf)) % BF16_SUBLANES == 0
    return pl.pallas_call(
        functools.partial(_mlp_kernel, final_norm=final_norm),
        grid=(m // tm, dff // tf),
        in_specs=[
            pl.BlockSpec((tm, d), lambda i, f: (i, 0)),
            pl.BlockSpec((tm, d), lambda i, f: (jnp.minimum(i + 1, last), 0)),
            pl.BlockSpec((1, d), lambda i, f: (0, 0)),
            _mod_spec(layer, 3, tpb, d),
            _mod_spec(layer, 4, tpb, d),
            _mod_spec(layer, 3, tpb, d, ahead=1, last_tile=last),
            _mod_spec(layer, 4, tpb, d, ahead=1, last_tile=last),
            _mod_spec(layer, 5, tpb, d),
            pl.BlockSpec((None, d, tf), lambda i, f: (layer, 0, f)),
            pl.BlockSpec((None, tf, d), lambda i, f: (layer, f, 0)),
            pl.BlockSpec((1, d), lambda i, f: (0, 0)),
        ],
        out_specs=pl.BlockSpec((tm, d), lambda i, f: (i, 0)),
        out_shape=jax.ShapeDtypeStruct((m, d), F32),
        scratch_shapes=[pltpu.VMEM((2, tm, d), BF16), pltpu.VMEM((tm, d), F32)],
        compiler_params=_params("arbitrary", "arbitrary"),
        name="mlp_final" if final_norm else "mlp",
    )(x, x, gain, mod, mod, mod, mod, mod, w_up, w_down, fin_gain)


def kernel(x, c, ada_w, ada_b, norm_mix, norm_mlp, fox_w_in, fox_b_f, fox_w_out, conv_w_in, conv_w,
           conv_w_out, mlp_w_up, mlp_w_down, final_norm):
    batch, seq, d = x.shape
    depth = ada_w.shape[0]
    dh = d // N_HEADS
    assert depth == 2 and dh == LANES and fox_w_in.shape[-1] == 3 * d + N_HEADS

    c_pad = jnp.pad(c, ((0, BF16_SUBLANES - batch), (0, 0)))
    mod = _ada_modulation(c_pad, ada_w, ada_b)
    mod = mod[:, :batch].reshape(depth, batch, N_MOD, 1, d)

    xf = x.reshape(batch * seq, d)
    row = lambda v: v.reshape(1, d)

    b_f = jnp.pad(fox_b_f[0], (0, LANES - N_HEADS)).reshape(1, LANES)
    w_fox_in = fox_w_in[0].astype(BF16)
    qkv, f_logit, w_attn_out, w_conv_in, w_conv_out = _inproj(
        xf, row(norm_mix[0]), mod, 0, w_fox_in, mixer="fox", side=w_fox_in, seq=seq,
        out_scale=LOG2_E * dh ** -0.5, casts=(fox_w_out[0], conv_w_in[0], conv_w_out[0]))
    fcum = _forget_cumsum(f_logit, b_f, batch=batch, seq=seq)
    attn, w_up, w_down = _fox_attention(qkv, fcum, mlp_w_up, mlp_w_down, batch=batch, seq=seq, d=d)
    xf = _outproj(attn, w_attn_out, xf, mod, 0, seq=seq)
    xf = _mlp(xf, row(norm_mlp[0]), mod, 0, w_up, w_down, row(final_norm), seq=seq,
              final_norm=False)

    gated = _inproj(xf, row(norm_mix[1]), mod, 1, w_conv_in, mixer="conv", side=conv_w[0],
                    seq=seq)[0]
    xf = _outproj(gated, w_conv_out, xf, mod, 1, seq=seq)
    xf = _mlp(xf, row(norm_mlp[1]), mod, 1, w_up, w_down, row(final_norm), seq=seq,
              final_norm=True)
    return xf.reshape(batch, seq, d)
```

```python
import functools

import jax
import jax.numpy as jnp
from jax import lax
from jax.experimental import pallas as pl
from jax.experimental.pallas import tpu as pltpu

N_HEADS = 16
CONV_WIDTH = 3
N_MOD = 6
RMS_EPS = 1e-6
LOG2_E = 1.4426950408889634

LANES = 128
F32_SUBLANES = 8
BF16_SUBLANES = 16
V7X_VMEM_BYTES = 64 * 1024 * 1024
VMEM_LIMIT_BYTES = V7X_VMEM_BYTES - 8 * 1024 * 1024
MLP_VMEM_LIMIT_BYTES = V7X_VMEM_BYTES - 4 * 1024 * 1024
MLP_CHUNKS = 2

F32 = jnp.float32
BF16 = jnp.bfloat16


def _params(*semantics):
    return pltpu.CompilerParams(dimension_semantics=semantics, vmem_limit_bytes=VMEM_LIMIT_BYTES)


def _dot(a, b):
    return jnp.dot(a, b, preferred_element_type=F32)


def _norm_modulate(x, gain, shift, scale):
    inv = lax.rsqrt(jnp.mean(x * x, axis=-1, keepdims=True) + RMS_EPS)
    return (x * inv) * gain * (1.0 + scale) + shift


def _ada_kernel(c_ref, w_ref, b_ref, o_ref):
    c = c_ref[...]
    act = (c * jax.nn.sigmoid(c)).astype(BF16)
    o_ref[...] = _dot(act, w_ref[...].astype(BF16)) + b_ref[...]


def _ada_modulation(c_pad, ada_w, ada_b, tn=1024):
    depth, d, n = ada_w.shape
    rows = c_pad.shape[0]
    return pl.pallas_call(
        _ada_kernel,
        grid=(depth, n // tn),
        in_specs=[
            pl.BlockSpec((rows, d), lambda l, j: (0, 0)),
            pl.BlockSpec((None, d, tn), lambda l, j: (l, 0, j)),
            pl.BlockSpec((None, 1, tn), lambda l, j: (l, 0, j)),
        ],
        out_specs=pl.BlockSpec((None, rows, tn), lambda l, j: (l, 0, j)),
        out_shape=jax.ShapeDtypeStruct((depth, rows, n), F32),
        compiler_params=_params("parallel", "parallel"),
        name="ada_modulation",
    )(c_pad, ada_w, ada_b.reshape(depth, 1, n))


def _mod_spec(layer, which, tiles_per_batch, d, ahead=0, last_tile=None):
    def index(i, *_):
        tile = jnp.minimum(i + ahead, last_tile) if ahead else i
        return (layer, tile // tiles_per_batch, which, 0, 0)
    return pl.BlockSpec((None, None, None, 1, d), index)


def _inproj_kernel(*refs, mixer, out_scale, n_casts, tiles_per_seq):
    x0_ref, xn_ref, g_ref, sh0_ref, sc0_ref, shn_ref, scn_ref, w_ref, side_ref = refs[:9]
    cast_in, refs = refs[9:9 + n_casts], refs[9 + n_casts:]
    o_ref, refs = refs[0], refs[1:]
    if mixer == "fox":
        f_ref, refs = refs[0], refs[1:]
    cast_out, scratch = refs[:n_casts], refs[n_casts:]
    h_scr = scratch[0]
    for src, dst in zip(cast_in, cast_out):
        dst[...] = src[...].astype(dst.dtype)
    i = pl.program_id(0)
    slot = i % 2
    tm, d = xn_ref.shape

    @pl.when(i == 0)
    def _():
        h_scr[0] = _norm_modulate(x0_ref[...], g_ref[...], sh0_ref[...], sc0_ref[...]).astype(BF16)
        if mixer == "conv":
            scratch[2][...] = jnp.zeros_like(scratch[2])

    def column_group(grp):
        return _dot(h_scr[slot], w_ref[:, grp * d:(grp + 1) * d])

    def normalise_next(part, parts):
        rows = slice(part * (tm // parts), (part + 1) * (tm // parts))
        h_scr[1 - slot, rows, :] = _norm_modulate(xn_ref[rows, :], g_ref[...], shn_ref[...],
                                                  scn_ref[...]).astype(BF16)

    if mixer == "fox":
        o_ref[:, 0:d] = (column_group(0) * out_scale).astype(o_ref.dtype)
        normalise_next(0, 2)
        o_ref[:, d:2 * d] = column_group(1).astype(o_ref.dtype)
        normalise_next(1, 2)
        o_ref[:, 2 * d:3 * d] = column_group(2).astype(o_ref.dtype)
        lane = lax.broadcasted_iota(jnp.int32, (1, LANES), 1)
        f_ref[...] = _dot(h_scr[slot], jnp.where(lane < N_HEADS, side_ref[...], 0.0).astype(BF16))
    else:
        ext, carry = scratch[1], scratch[2]
        c_gate = column_group(1)
        normalise_next(0, 2)
        gated = c_gate * column_group(2)
        normalise_next(1, 2)
        halo = carry.shape[0]
        ext[0:halo, :] = jnp.where(i % tiles_per_seq == 0, 0.0, carry[...])
        ext[halo:halo + tm, :] = gated
        carry[...] = gated[tm - halo:tm, :]
        y = None
        for tap in range(CONV_WIDTH):
            start = halo - (CONV_WIDTH - 1 - tap)
            term = side_ref[tap:tap + 1, :] * ext[start:start + tm, :]
            y = term if y is None else y + term
        o_ref[...] = (column_group(0) * y).astype(o_ref.dtype)


def _inproj(x, gain, mod, layer, w, *, mixer, seq, side, out_scale=1.0, tm=256, casts=()):
    m, d = x.shape
    n = 3 * d
    tpb = seq // tm
    last = m // tm - 1
    steps = m // tm
    slab = lambda a: pl.BlockSpec((a.shape[0] // steps, a.shape[1]), lambda i: (i, 0))
    in_specs = [
        pl.BlockSpec((tm, d), lambda i: (0, 0)),
        pl.BlockSpec((tm, d), lambda i: (jnp.minimum(i + 1, last), 0)),
        pl.BlockSpec((1, d), lambda i: (0, 0)),
        _mod_spec(layer, 0, tpb, d),
        _mod_spec(layer, 1, tpb, d),
        _mod_spec(layer, 0, tpb, d, ahead=1, last_tile=last),
        _mod_spec(layer, 1, tpb, d, ahead=1, last_tile=last),
        pl.BlockSpec((d, n), lambda i: (0, 0), pipeline_mode=pl.Buffered(1)),
    ]
    args = [x, x, gain, mod, mod, mod, mod, w, side]
    scratch = [pltpu.VMEM((2, tm, d), BF16)]
    if mixer == "fox":
        in_specs.append(pl.BlockSpec((d, LANES), lambda i: (0, n // LANES)))
        out_specs = [pl.BlockSpec((tm, n), lambda i: (i, 0)),
                     pl.BlockSpec((tm, LANES), lambda i: (i, 0))]
        out_shape = [jax.ShapeDtypeStruct((m, n), BF16), jax.ShapeDtypeStruct((m, LANES), F32)]
    else:
        in_specs.append(pl.BlockSpec(side.shape, lambda i: (0, 0)))
        out_specs = [pl.BlockSpec((tm, d), lambda i: (i, 0))]
        out_shape = [jax.ShapeDtypeStruct((m, d), BF16)]
        scratch += [pltpu.VMEM((F32_SUBLANES + tm, d), F32), pltpu.VMEM((F32_SUBLANES, d), F32)]
    for a in casts:
        assert (a.shape[0] // steps) % BF16_SUBLANES == 0
        in_specs.append(slab(a))
        args.append(a)
        out_specs.append(slab(a))
        out_shape.append(jax.ShapeDtypeStruct(a.shape, BF16))
    kern = functools.partial(_inproj_kernel, mixer=mixer, out_scale=out_scale, n_casts=len(casts),
                             tiles_per_seq=tpb)
    return pl.pallas_call(
        kern,
        grid=(m // tm,),
        in_specs=in_specs,
        out_specs=out_specs,
        out_shape=out_shape,
        scratch_shapes=scratch,
        compiler_params=_params("arbitrary"),
        name="inproj_" + mixer,
    )(*args)


def _split3(x):
    hi = x.astype(BF16).astype(F32)
    r = x - hi
    mid = r.astype(BF16).astype(F32)
    lo = (r - mid).astype(BF16).astype(F32)
    return hi, mid, lo


def _forget_cumsum_kernel(fl_ref, bf_ref, o_ref, carry):
    @pl.when(pl.program_id(1) == 0)
    def _():
        carry[...] = jnp.zeros_like(carry)

    z = fl_ref[...] + bf_ref[...]
    log_f = jnp.minimum(z, 0.0) - jnp.log1p(jnp.exp(-jnp.abs(z)))
    tc = z.shape[0]
    row = lax.broadcasted_iota(jnp.int32, (tc, tc), 0)
    col = lax.broadcasted_iota(jnp.int32, (tc, tc), 1)
    tri = jnp.where(col <= row, 1.0, 0.0).astype(BF16)
    hi, mid, lo = _split3(log_f)
    csum = (_dot(tri, hi.astype(BF16)) + _dot(tri, mid.astype(BF16))) + _dot(tri, lo.astype(BF16))
    out = csum + carry[...]
    carry[...] = out[tc - 1:tc, :]
    neg_f = (out * -LOG2_E).T[0:N_HEADS, :]
    for plane, term in enumerate(_split3(neg_f)):
        o_ref[plane] = term


def _forget_cumsum(f_logit, b_f_pad, *, batch, seq, tc=512):
    fl = f_logit.reshape(batch, seq, LANES)
    return pl.pallas_call(
        _forget_cumsum_kernel,
        grid=(batch, seq // tc),
        in_specs=[
            pl.BlockSpec((None, tc, LANES), lambda b, i: (b, i, 0)),
            pl.BlockSpec((1, LANES), lambda b, i: (0, 0)),
        ],
        out_specs=pl.BlockSpec((None, 3, N_HEADS, tc), lambda b, i: (b, 0, 0, i)),
        out_shape=jax.ShapeDtypeStruct((batch, 3, N_HEADS, seq), F32),
        scratch_shapes=[pltpu.VMEM((1, LANES), F32)],
        compiler_params=_params("parallel", "arbitrary"),
        name="forget_cumsum",
    )(fl, b_f_pad)


def _fox_attn_kernel(q_ref, k_ref, v_ref, f_ref, wu_ref, wd_ref, o_ref, wu_bf_ref, wd_bf_ref,
                     kaug, vt, qt, *, tq, tk):
    wu_bf_ref[...] = wu_ref[...].astype(BF16)
    wd_bf_ref[...] = wd_ref[...].astype(BF16)
    seq, dh = k_ref.shape
    head = pl.ds(pl.program_id(1), 1)
    sub = lax.broadcasted_iota(jnp.int32, (F32_SUBLANES, 1), 0)

    for c in range(seq // tk):
        rows = slice(c * tk, (c + 1) * tk)
        f_rows = jnp.where(sub == 0, f_ref[0, head, rows], jnp.where(
            sub == 1, f_ref[1, head, rows], jnp.where(sub == 2, f_ref[2, head, rows], 0.0)))
        f_cols = jnp.concatenate([f_rows, jnp.zeros((LANES - F32_SUBLANES, tk), F32)],
                                 axis=0).T
        kaug[rows, 0:dh] = k_ref[rows, :]
        kaug[rows, dh:dh + LANES] = f_cols.astype(BF16)
        vt[0:dh, rows] = v_ref[rows, :].T
        qt[0:dh, rows] = q_ref[rows, :].T
    pad_row = lax.broadcasted_iota(jnp.int32, (BF16_SUBLANES, 1), 0)
    vt[dh:dh + BF16_SUBLANES, :] = jnp.broadcast_to(
        jnp.where(pad_row == 0, 1.0, 0.0).astype(BF16), (BF16_SUBLANES, seq))
    aug_row = lax.broadcasted_iota(jnp.int32, (LANES, 1), 0)
    qt[dh:dh + LANES, :] = jnp.broadcast_to(
        jnp.where(aug_row < 3, 1.0, 0.0).astype(BF16), (LANES, seq))

    key_in_chunk = lax.broadcasted_iota(jnp.int32, (tk, tq), 0)
    query_in_block = lax.broadcasted_iota(jnp.int32, (tk, tq), 1)
    blocks = [(qi, j) for qi in range(seq // tq) for j in range(pl.cdiv((qi + 1) * tq, tk))]

    def logits(qi, j):
        return _dot(kaug[j * tk:(j + 1) * tk, :], qt[:, qi * tq:(qi + 1) * tq])

    s_next = logits(*blocks[0])
    for t, (qi, j) in enumerate(blocks):
        s = s_next
        if t + 1 < len(blocks):
            s_next = logits(*blocks[t + 1])
        q_lo, k_lo = qi * tq, j * tk
        if j == 0:
            m = jnp.full((1, tq), -jnp.inf, F32)
            acc = jnp.zeros((dh + BF16_SUBLANES, tq), F32)
        if k_lo + tk - 1 > q_lo:
            s = jnp.where(key_in_chunk + k_lo <= query_in_block + q_lo, s, -jnp.inf)
        m_new = jnp.maximum(m, jnp.max(s, axis=0, keepdims=True))
        alpha = jnp.exp2(m - m_new)
        p = jnp.exp2(s - m_new)
        acc = alpha * acc + _dot(vt[:, k_lo:k_lo + tk], p.astype(BF16))
        m = m_new
        if k_lo + tk >= q_lo + tq:
            inv_l = 1.0 / acc[dh:dh + 1, :]
            o_ref[q_lo:q_lo + tq, :] = (acc[0:dh, :] * inv_l).T.astype(o_ref.dtype)


def _fox_attention(qkv, f_terms, w_up, w_down, *, batch, seq, d, tq=512, tk=512):
    dh = d // N_HEADS
    qkv3 = qkv.reshape(batch, seq, 3 * d)
    head_cols = lambda part: pl.BlockSpec((None, seq, dh), lambda b, h: (b, 0, part * N_HEADS + h))
    steps = batch * N_HEADS
    wu2 = w_up.reshape(-1, w_up.shape[-1])
    wd2 = w_down.reshape(-1, w_down.shape[-1])
    slab = lambda w2: pl.BlockSpec((w2.shape[0] // steps, w2.shape[1]),
                                   lambda b, h: (b * N_HEADS + h, 0))
    attn, wu_bf, wd_bf = pl.pallas_call(
        functools.partial(_fox_attn_kernel, tq=tq, tk=tk),
        grid=(batch, N_HEADS),
        in_specs=[
            head_cols(0),
            head_cols(1),
            head_cols(2),
            pl.BlockSpec((None, 3, N_HEADS, seq), lambda b, h: (b, 0, 0, 0)),
            slab(wu2),
            slab(wd2),
        ],
        out_specs=[head_cols(0), slab(wu2), slab(wd2)],
        out_shape=[jax.ShapeDtypeStruct((batch, seq, d), BF16),
                   jax.ShapeDtypeStruct(wu2.shape, BF16),
                   jax.ShapeDtypeStruct(wd2.shape, BF16)],
        scratch_shapes=[pltpu.VMEM((seq, dh + LANES), BF16),
                        pltpu.VMEM((dh + BF16_SUBLANES, seq), BF16),
                        pltpu.VMEM((dh + LANES, seq), BF16)],
        compiler_params=_params("parallel", "parallel"),
        name="fox_attention",
    )(qkv3, qkv3, qkv3, f_terms, wu2, wd2)
    return attn.reshape(batch * seq, d), wu_bf.reshape(w_up.shape), wd_bf.reshape(w_down.shape)


def _outproj_kernel(a_ref, w_ref, x_ref, g_ref, o_ref):
    o_ref[...] = x_ref[...] + g_ref[...] * _dot(a_ref[...], w_ref[...])


def _outproj(a, w, x, mod, layer, *, seq, tm=512):
    m, d = x.shape
    tpb = seq // tm
    return pl.pallas_call(
        _outproj_kernel,
        grid=(m // tm,),
        in_specs=[
            pl.BlockSpec((tm, d), lambda i: (i, 0)),
            pl.BlockSpec((d, d), lambda i: (0, 0)),
            pl.BlockSpec((tm, d), lambda i: (i, 0)),
            _mod_spec(layer, 2, tpb, d),
        ],
        out_specs=pl.BlockSpec((tm, d), lambda i: (i, 0)),
        out_shape=jax.ShapeDtypeStruct((m, d), F32),
        compiler_params=_params("parallel"),
        name="mixer_outproj",
    )(a, w, x, mod)


def _mlp_kernel(x0_ref, xs_ref, xn_ref, g_ref, sh_ref, sc_ref, shn_ref, scn_ref, gate_ref, wu_ref,
                wd_ref, fin_ref, o_ref, h_scr, *, final_norm):
    i = pl.program_id(0)
    f = pl.program_id(1)
    slot = i % 2
    rows_per_step = xs_ref.shape[0]
    chunk = wu_ref.shape[1] // MLP_CHUNKS
    half = chunk // 2

    @pl.when(jnp.logical_and(i == 0, f == 0))
    def _():
        h_scr[0] = _norm_modulate(x0_ref[...], g_ref[...], sh_ref[...], sc_ref[...]).astype(BF16)

    row0 = pl.multiple_of(f * rows_per_step, rows_per_step)

    def hidden_block(first):
        if not first:
            o_ref[pl.ds(row0, rows_per_step), :] += xs_ref[...]
        for c in range(MLP_CHUNKS):
            lo = c * chunk
            u_lo = jnp.maximum(_dot(h_scr[slot], wu_ref[:, lo:lo + half]), 0.0)
            if c == 0:
                h_scr[1 - slot, pl.ds(row0, rows_per_step), :] = _norm_modulate(
                    xn_ref[...], g_ref[...], shn_ref[...], scn_ref[...]).astype(BF16)
            u_hi = jnp.maximum(_dot(h_scr[slot], wu_ref[:, lo + half:lo + chunk]), 0.0)
            u = jnp.concatenate([u_lo, u_hi], axis=1)
            part = gate_ref[...] * _dot((u * u).astype(BF16), wd_ref[lo:lo + chunk, :])
            if first and c == 0:
                o_ref[...] = part
            else:
                o_ref[...] += part
        if first:
            o_ref[pl.ds(row0, rows_per_step), :] += xs_ref[...]

    pl.when(f == 0)(functools.partial(hidden_block, True))
    pl.when(f > 0)(functools.partial(hidden_block, False))

    if final_norm:
        @pl.when(f == pl.num_programs(1) - 1)
        def _():
            y = o_ref[...]
            inv = lax.rsqrt(jnp.mean(y * y, axis=-1, keepdims=True) + RMS_EPS)
            o_ref[...] = (y * inv) * fin_ref[...]


def _mlp(x, gain, mod, layer, w_up, w_down, fin_gain, *, seq, final_norm, tm=512, tf=2048):
    m, d = x.shape
    dff = w_up.shape[2]
    nf = dff // tf
    tpb = seq // tm
    last = m // tm - 1
    rows_per_step = tm // nf
    assert (rows_per_step // MLP_CHUNKS) % BF16_SUBLANES == 0
    return pl.pallas_call(
        functools.partial(_mlp_kernel, final_norm=final_norm),
        grid=(m // tm, nf),
        in_specs=[
            pl.BlockSpec((tm, d), lambda i, f: (0, 0), pipeline_mode=pl.Buffered(1)),
            pl.BlockSpec((rows_per_step, d), lambda i, f: (i * nf + f, 0)),
            pl.BlockSpec((rows_per_step, d), lambda i, f: (jnp.minimum(i + 1, last) * nf + f, 0)),
            pl.BlockSpec((1, d), lambda i, f: (0, 0)),
            _mod_spec(layer, 3, tpb, d),
            _mod_spec(layer, 4, tpb, d),
            _mod_spec(layer, 3, tpb, d, ahead=1, last_tile=last),
            _mod_spec(layer, 4, tpb, d, ahead=1, last_tile=last),
            _mod_spec(layer, 5, tpb, d),
            pl.BlockSpec((None, d, tf), lambda i, f: (layer, 0, f)),
            pl.BlockSpec((None, tf, d), lambda i, f: (layer, f, 0)),
            pl.BlockSpec((1, d), lambda i, f: (0, 0)),
        ],
        out_specs=pl.BlockSpec((tm, d), lambda i, f: (i, 0)),
        out_shape=jax.ShapeDtypeStruct((m, d), F32),
        scratch_shapes=[pltpu.VMEM((2, tm, d), BF16)],
        compiler_params=pltpu.CompilerParams(dimension_semantics=("arbitrary", "arbitrary"),
                                             vmem_limit_bytes=MLP_VMEM_LIMIT_BYTES),
        name="mlp_final" if final_norm else "mlp",
    )(x, x, x, gain, mod, mod, mod, mod, mod, w_up, w_down, fin_gain)


def kernel(x, c, ada_w, ada_b, norm_mix, norm_mlp, fox_w_in, fox_b_f, fox_w_out, conv_w_in, conv_w,
           conv_w_out, mlp_w_up, mlp_w_down, final_norm):
    batch, seq, d = x.shape
    depth = ada_w.shape[0]
    dh = d // N_HEADS
    assert depth == 2 and dh == LANES and fox_w_in.shape[-1] == 3 * d + N_HEADS

    c_pad = jnp.pad(c, ((0, BF16_SUBLANES - batch), (0, 0)))
    mod = _ada_modulation(c_pad, ada_w, ada_b)
    mod = mod[:, :batch].reshape(depth, batch, N_MOD, 1, d)

    xf = x.reshape(batch * seq, d)
    row = lambda v: v.reshape(1, d)

    b_f = jnp.pad(fox_b_f[0], (0, LANES - N_HEADS)).reshape(1, LANES)
    w_fox_in = fox_w_in[0].astype(BF16)
    qkv, f_logit, w_attn_out, w_conv_in, w_conv_out = _inproj(
        xf, row(norm_mix[0]), mod, 0, w_fox_in, mixer="fox", side=w_fox_in, seq=seq,
        out_scale=LOG2_E * dh ** -0.5, casts=(fox_w_out[0], conv_w_in[0], conv_w_out[0]))
    fcum = _forget_cumsum(f_logit, b_f, batch=batch, seq=seq)
    attn, w_up, w_down = _fox_attention(qkv, fcum, mlp_w_up, mlp_w_down, batch=batch, seq=seq, d=d)
    xf = _outproj(attn, w_attn_out, xf, mod, 0, seq=seq)
    xf = _mlp(xf, row(norm_mlp[0]), mod, 0, w_up, w_down, row(final_norm), seq=seq,
              final_norm=False)

    gated = _inproj(xf, row(norm_mix[1]), mod, 1, w_conv_in, mixer="conv", side=conv_w[0],
                    seq=seq)[0]
    xf = _outproj(gated, w_conv_out, xf, mod, 1, seq=seq)
    xf = _mlp(xf, row(norm_mlp[1]), mod, 1, w_up, w_down, row(final_norm), seq=seq,
              final_norm=True)
    return xf.reshape(batch, seq, d)
```

```python
import functools

import jax
import jax.numpy as jnp
from jax import lax
from jax.experimental import pallas as pl
from jax.experimental.pallas import tpu as pltpu

N_HEADS = 16
CONV_WIDTH = 3
N_MOD = 6
RMS_EPS = 1e-6
LOG2_E = 1.4426950408889634

LANES = 128
F32_SUBLANES = 8
BF16_SUBLANES = 16
V7X_VMEM_BYTES = 64 * 1024 * 1024
VMEM_LIMIT_BYTES = V7X_VMEM_BYTES - 8 * 1024 * 1024
MLP_VMEM_LIMIT_BYTES = V7X_VMEM_BYTES - 4 * 1024 * 1024
MLP_CHUNKS = 2

F32 = jnp.float32
BF16 = jnp.bfloat16


def _params(*semantics):
    return pltpu.CompilerParams(dimension_semantics=semantics, vmem_limit_bytes=VMEM_LIMIT_BYTES)


def _dot(a, b):
    return jnp.dot(a, b, preferred_element_type=F32)


def _norm_modulate(x, gain, shift, scale):
    inv = lax.rsqrt(jnp.mean(x * x, axis=-1, keepdims=True) + RMS_EPS)
    return (x * inv) * gain * (1.0 + scale) + shift


def _ada_kernel(c_ref, w_ref, b_ref, o_ref):
    c = c_ref[...]
    act = (c * jax.nn.sigmoid(c)).astype(BF16)
    o_ref[...] = _dot(act, w_ref[...].astype(BF16)) + b_ref[...]


def _ada_modulation(c_pad, ada_w, ada_b, tn=1024):
    depth, d, n = ada_w.shape
    rows = c_pad.shape[0]
    return pl.pallas_call(
        _ada_kernel,
        grid=(depth, n // tn),
        in_specs=[
            pl.BlockSpec((rows, d), lambda l, j: (0, 0)),
            pl.BlockSpec((None, d, tn), lambda l, j: (l, 0, j)),
            pl.BlockSpec((None, 1, tn), lambda l, j: (l, 0, j)),
        ],
        out_specs=pl.BlockSpec((None, rows, tn), lambda l, j: (l, 0, j)),
        out_shape=jax.ShapeDtypeStruct((depth, rows, n), F32),
        compiler_params=_params("parallel", "parallel"),
        name="ada_modulation",
    )(c_pad, ada_w, ada_b.reshape(depth, 1, n))


def _mod_spec(layer, which, tiles_per_batch, d, ahead=0, last_tile=None):
    def index(i, *_):
        tile = jnp.minimum(i + ahead, last_tile) if ahead else i
        return (layer, tile // tiles_per_batch, which, 0, 0)
    return pl.BlockSpec((None, None, None, 1, d), index)


def _inproj_kernel(*refs, mixer, out_scale, n_casts, tiles_per_seq):
    x0_ref, xn_ref, g_ref, sh0_ref, sc0_ref, shn_ref, scn_ref, w_ref, side_ref = refs[:9]
    refs = refs[9:]
    if mixer == "fox":
        bf_ref, refs = refs[0], refs[1:]
    cast_in, refs = refs[:n_casts], refs[n_casts:]
    o_ref, refs = refs[0], refs[1:]
    if mixer == "fox":
        f_ref, refs = refs[0], refs[1:]
    cast_out, scratch = refs[:n_casts], refs[n_casts:]
    h_scr = scratch[0]
    for src, dst in zip(cast_in, cast_out):
        dst[...] = src[...].astype(dst.dtype)
    i = pl.program_id(0)
    slot = i % 2
    tm, d = xn_ref.shape

    @pl.when(i == 0)
    def _():
        h_scr[0] = _norm_modulate(x0_ref[...], g_ref[...], sh0_ref[...], sc0_ref[...]).astype(BF16)
        scratch[2][...] = jnp.zeros_like(scratch[2])
        if mixer == "fox":
            row = lax.broadcasted_iota(jnp.int32, (tm, tm), 0)
            col = lax.broadcasted_iota(jnp.int32, (tm, tm), 1)
            scratch[1][...] = jnp.where(col <= row, 1.0, 0.0).astype(BF16)

    def column_group(grp):
        return _dot(h_scr[slot], w_ref[:, grp * d:(grp + 1) * d])

    def normalise_next(part, parts):
        rows = slice(part * (tm // parts), (part + 1) * (tm // parts))
        h_scr[1 - slot, rows, :] = _norm_modulate(xn_ref[rows, :], g_ref[...], shn_ref[...],
                                                  scn_ref[...]).astype(BF16)

    if mixer == "fox":
        tri, carry = scratch[1], scratch[2]
        lane = lax.broadcasted_iota(jnp.int32, (1, LANES), 1)
        z = _dot(h_scr[slot], jnp.where(lane < N_HEADS, side_ref[...], 0.0).astype(BF16))
        z = z + bf_ref[...]
        log_f = jnp.minimum(z, 0.0) - jnp.log1p(jnp.exp(-jnp.abs(z)))
        o_ref[:, 0:d] = (column_group(0) * out_scale).astype(o_ref.dtype)
        normalise_next(0, 2)
        sums = _dot(tri[...], jnp.concatenate(_split3(log_f), axis=1).astype(BF16))
        o_ref[:, d:2 * d] = column_group(1).astype(o_ref.dtype)
        normalise_next(1, 2)
        csum = (sums[:, 0:LANES] + sums[:, LANES:2 * LANES]) + sums[:, 2 * LANES:]
        csum = csum + jnp.where(i % tiles_per_seq == 0, 0.0, carry[...])
        carry[...] = csum[tm - 1:tm, :]
        neg_f = (csum * -LOG2_E).T[0:N_HEADS, :]
        for plane, term in enumerate(_split3(neg_f)):
            f_ref[plane] = term
        o_ref[:, 2 * d:3 * d] = column_group(2).astype(o_ref.dtype)
    else:
        ext, carry = scratch[1], scratch[2]
        c_gate = column_group(1)
        normalise_next(0, 2)
        gated = c_gate * column_group(2)
        normalise_next(1, 2)
        halo = carry.shape[0]
        ext[0:halo, :] = jnp.where(i % tiles_per_seq == 0, 0.0, carry[...])
        ext[halo:halo + tm, :] = gated
        carry[...] = gated[tm - halo:tm, :]
        y = None
        for tap in range(CONV_WIDTH):
            start = halo - (CONV_WIDTH - 1 - tap)
            term = side_ref[tap:tap + 1, :] * ext[start:start + tm, :]
            y = term if y is None else y + term
        o_ref[...] = (column_group(0) * y).astype(o_ref.dtype)


def _inproj(x, gain, mod, layer, w, *, mixer, seq, side, out_scale=1.0, tm=256, casts=(),
            forget_bias=None):
    m, d = x.shape
    n = 3 * d
    tpb = seq // tm
    last = m // tm - 1
    steps = m // tm
    slab = lambda a: pl.BlockSpec((a.shape[0] // steps, a.shape[1]), lambda i: (i, 0))
    in_specs = [
        pl.BlockSpec((tm, d), lambda i: (0, 0)),
        pl.BlockSpec((tm, d), lambda i: (jnp.minimum(i + 1, last), 0)),
        pl.BlockSpec((1, d), lambda i: (0, 0)),
        _mod_spec(layer, 0, tpb, d),
        _mod_spec(layer, 1, tpb, d),
        _mod_spec(layer, 0, tpb, d, ahead=1, last_tile=last),
        _mod_spec(layer, 1, tpb, d, ahead=1, last_tile=last),
        pl.BlockSpec((d, n), lambda i: (0, 0), pipeline_mode=pl.Buffered(1)),
    ]
    args = [x, x, gain, mod, mod, mod, mod, w, side]
    scratch = [pltpu.VMEM((2, tm, d), BF16)]
    if mixer == "fox":
        in_specs += [pl.BlockSpec((d, LANES), lambda i: (0, n // LANES)),
                     pl.BlockSpec((1, LANES), lambda i: (0, 0))]
        args.append(forget_bias)
        out_specs = [pl.BlockSpec((tm, n), lambda i: (i, 0)),
                     pl.BlockSpec((None, 3, N_HEADS, tm), lambda i: (i // tpb, 0, 0, i % tpb))]
        out_shape = [jax.ShapeDtypeStruct((m, n), BF16),
                     jax.ShapeDtypeStruct((m // seq, 3, N_HEADS, seq), F32)]
        scratch += [pltpu.VMEM((tm, tm), BF16), pltpu.VMEM((1, LANES), F32)]
    else:
        in_specs.append(pl.BlockSpec(side.shape, lambda i: (0, 0)))
        out_specs = [pl.BlockSpec((tm, d), lambda i: (i, 0))]
        out_shape = [jax.ShapeDtypeStruct((m, d), BF16)]
        scratch += [pltpu.VMEM((F32_SUBLANES + tm, d), F32), pltpu.VMEM((F32_SUBLANES, d), F32)]
    for a in casts:
        assert (a.shape[0] // steps) % BF16_SUBLANES == 0
        in_specs.append(slab(a))
        args.append(a)
        out_specs.append(slab(a))
        out_shape.append(jax.ShapeDtypeStruct(a.shape, BF16))
    kern = functools.partial(_inproj_kernel, mixer=mixer, out_scale=out_scale, n_casts=len(casts),
                             tiles_per_seq=tpb)
    return pl.pallas_call(
        kern,
        grid=(m // tm,),
        in_specs=in_specs,
        out_specs=out_specs,
        out_shape=out_shape,
        scratch_shapes=scratch,
        compiler_params=_params("arbitrary"),
        name="inproj_" + mixer,
    )(*args)


def _split3(x):
    hi = x.astype(BF16).astype(F32)
    r = x - hi
    mid = r.astype(BF16).astype(F32)
    lo = (r - mid).astype(BF16).astype(F32)
    return hi, mid, lo


def _fox_attn_kernel(q_ref, k_ref, v_ref, f_ref, wu_ref, wd_ref, o_ref, wu_bf_ref, wd_bf_ref,
                     kaug, vt, qt, *, tq, tk):
    wu_bf_ref[...] = wu_ref[...].astype(BF16)
    wd_bf_ref[...] = wd_ref[...].astype(BF16)
    seq, dh = k_ref.shape
    head = pl.ds(pl.program_id(1), 1)
    sub = lax.broadcasted_iota(jnp.int32, (F32_SUBLANES, 1), 0)

    for c in range(seq // tk):
        rows = slice(c * tk, (c + 1) * tk)
        f_rows = jnp.where(sub == 0, f_ref[0, head, rows], jnp.where(
            sub == 1, f_ref[1, head, rows], jnp.where(sub == 2, f_ref[2, head, rows], 0.0)))
        f_cols = jnp.concatenate([f_rows, jnp.zeros((LANES - F32_SUBLANES, tk), F32)],
                                 axis=0).T
        kaug[rows, 0:dh] = k_ref[rows, :]
        kaug[rows, dh:dh + LANES] = f_cols.astype(BF16)
        vt[0:dh, rows] = v_ref[rows, :].T
        qt[0:dh, rows] = q_ref[rows, :].T
    pad_row = lax.broadcasted_iota(jnp.int32, (BF16_SUBLANES, 1), 0)
    vt[dh:dh + BF16_SUBLANES, :] = jnp.broadcast_to(
        jnp.where(pad_row == 0, 1.0, 0.0).astype(BF16), (BF16_SUBLANES, seq))
    aug_row = lax.broadcasted_iota(jnp.int32, (LANES, 1), 0)
    qt[dh:dh + LANES, :] = jnp.broadcast_to(
        jnp.where(aug_row < 3, 1.0, 0.0).astype(BF16), (LANES, seq))

    key_in_chunk = lax.broadcasted_iota(jnp.int32, (tk, tq), 0)
    query_in_block = lax.broadcasted_iota(jnp.int32, (tk, tq), 1)
    blocks = [(qi, j) for qi in range(seq // tq) for j in range(pl.cdiv((qi + 1) * tq, tk))]

    def logits(qi, j):
        return _dot(kaug[j * tk:(j + 1) * tk, :], qt[:, qi * tq:(qi + 1) * tq])

    s_next = logits(*blocks[0])
    for t, (qi, j) in enumerate(blocks):
        s = s_next
        if t + 1 < len(blocks):
            s_next = logits(*blocks[t + 1])
        q_lo, k_lo = qi * tq, j * tk
        if j == 0:
            m = jnp.full((1, tq), -jnp.inf, F32)
            acc = jnp.zeros((dh + BF16_SUBLANES, tq), F32)
        if k_lo + tk - 1 > q_lo:
            s = jnp.where(key_in_chunk + k_lo <= query_in_block + q_lo, s, -jnp.inf)
        m_new = jnp.maximum(m, jnp.max(s, axis=0, keepdims=True))
        alpha = jnp.exp2(m - m_new)
        p = jnp.exp2(s - m_new)
        acc = alpha * acc + _dot(vt[:, k_lo:k_lo + tk], p.astype(BF16))
        m = m_new
        if k_lo + tk >= q_lo + tq:
            inv_l = 1.0 / acc[dh:dh + 1, :]
            o_ref[q_lo:q_lo + tq, :] = (acc[0:dh, :] * inv_l).T.astype(o_ref.dtype)


def _fox_attention(qkv, f_terms, w_up, w_down, *, batch, seq, d, tq=512, tk=512):
    dh = d // N_HEADS
    qkv3 = qkv.reshape(batch, seq, 3 * d)
    head_cols = lambda part: pl.BlockSpec((None, seq, dh), lambda b, h: (b, 0, part * N_HEADS + h))
    steps = batch * N_HEADS
    wu2 = w_up.reshape(-1, w_up.shape[-1])
    wd2 = w_down.reshape(-1, w_down.shape[-1])
    slab = lambda w2: pl.BlockSpec((w2.shape[0] // steps, w2.shape[1]),
                                   lambda b, h: (b * N_HEADS + h, 0))
    attn, wu_bf, wd_bf = pl.pallas_call(
        functools.partial(_fox_attn_kernel, tq=tq, tk=tk),
        grid=(batch, N_HEADS),
        in_specs=[
            head_cols(0),
            head_cols(1),
            head_cols(2),
            pl.BlockSpec((None, 3, N_HEADS, seq), lambda b, h: (b, 0, 0, 0)),
            slab(wu2),
            slab(wd2),
        ],
        out_specs=[head_cols(0), slab(wu2), slab(wd2)],
        out_shape=[jax.ShapeDtypeStruct((batch, seq, d), BF16),
                   jax.ShapeDtypeStruct(wu2.shape, BF16),
                   jax.ShapeDtypeStruct(wd2.shape, BF16)],
        scratch_shapes=[pltpu.VMEM((seq, dh + LANES), BF16),
                        pltpu.VMEM((dh + BF16_SUBLANES, seq), BF16),
                        pltpu.VMEM((dh + LANES, seq), BF16)],
        compiler_params=_params("parallel", "parallel"),
        name="fox_attention",
    )(qkv3, qkv3, qkv3, f_terms, wu2, wd2)
    return attn.reshape(batch * seq, d), wu_bf.reshape(w_up.shape), wd_bf.reshape(w_down.shape)


def _outproj_kernel(a_ref, w_ref, x_ref, g_ref, o_ref):
    o_ref[...] = x_ref[...] + g_ref[...] * _dot(a_ref[...], w_ref[...])


def _outproj(a, w, x, mod, layer, *, seq, tm=512):
    m, d = x.shape
    tpb = seq // tm
    return pl.pallas_call(
        _outproj_kernel,
        grid=(m // tm,),
        in_specs=[
            pl.BlockSpec((tm, d), lambda i: (i, 0)),
            pl.BlockSpec((d, d), lambda i: (0, 0)),
            pl.BlockSpec((tm, d), lambda i: (i, 0)),
            _mod_spec(layer, 2, tpb, d),
        ],
        out_specs=pl.BlockSpec((tm, d), lambda i: (i, 0)),
        out_shape=jax.ShapeDtypeStruct((m, d), F32),
        compiler_params=_params("parallel"),
        name="mixer_outproj",
    )(a, w, x, mod)


def _mlp_kernel(x0_ref, xs_ref, xn_ref, g_ref, sh_ref, sc_ref, shn_ref, scn_ref, gate_ref, wu_ref,
                wd_ref, fin_ref, o_ref, h_scr, *, final_norm):
    i = pl.program_id(0)
    f = pl.program_id(1)
    slot = i % 2
    rows_per_step = xs_ref.shape[0]
    chunk = wu_ref.shape[1] // MLP_CHUNKS
    half = chunk // 2

    @pl.when(jnp.logical_and(i == 0, f == 0))
    def _():
        h_scr[0] = _norm_modulate(x0_ref[...], g_ref[...], sh_ref[...], sc_ref[...]).astype(BF16)

    row0 = pl.multiple_of(f * rows_per_step, rows_per_step)

    def hidden_block(first):
        if not first:
            o_ref[pl.ds(row0, rows_per_step), :] += xs_ref[...]
        for c in range(MLP_CHUNKS):
            lo = c * chunk
            u_lo = jnp.maximum(_dot(h_scr[slot], wu_ref[:, lo:lo + half]), 0.0)
            if c == 0:
                h_scr[1 - slot, pl.ds(row0, rows_per_step), :] = _norm_modulate(
                    xn_ref[...], g_ref[...], shn_ref[...], scn_ref[...]).astype(BF16)
            u_hi = jnp.maximum(_dot(h_scr[slot], wu_ref[:, lo + half:lo + chunk]), 0.0)
            u = jnp.concatenate([u_lo, u_hi], axis=1)
            part = gate_ref[...] * _dot((u * u).astype(BF16), wd_ref[lo:lo + chunk, :])
            if first and c == 0:
                o_ref[...] = part
            else:
                o_ref[...] += part
        if first:
            o_ref[pl.ds(row0, rows_per_step), :] += xs_ref[...]

    pl.when(f == 0)(functools.partial(hidden_block, True))
    pl.when(f > 0)(functools.partial(hidden_block, False))

    if final_norm:
        @pl.when(f == pl.num_programs(1) - 1)
        def _():
            y = o_ref[...]
            inv = lax.rsqrt(jnp.mean(y * y, axis=-1, keepdims=True) + RMS_EPS)
            o_ref[...] = (y * inv) * fin_ref[...]


def _mlp(x, gain, mod, layer, w_up, w_down, fin_gain, *, seq, final_norm, tm=512, tf=2048):
    m, d = x.shape
    dff = w_up.shape[2]
    nf = dff // tf
    tpb = seq // tm
    last = m // tm - 1
    rows_per_step = tm // nf
    assert (rows_per_step // MLP_CHUNKS) % BF16_SUBLANES == 0
    return pl.pallas_call(
        functools.partial(_mlp_kernel, final_norm=final_norm),
        grid=(m // tm, nf),
        in_specs=[
            pl.BlockSpec((tm, d), lambda i, f: (0, 0), pipeline_mode=pl.Buffered(1)),
            pl.BlockSpec((rows_per_step, d), lambda i, f: (i * nf + f, 0)),
            pl.BlockSpec((rows_per_step, d), lambda i, f: (jnp.minimum(i + 1, last) * nf + f, 0)),
            pl.BlockSpec((1, d), lambda i, f: (0, 0)),
            _mod_spec(layer, 3, tpb, d),
            _mod_spec(layer, 4, tpb, d),
            _mod_spec(layer, 3, tpb, d, ahead=1, last_tile=last),
            _mod_spec(layer, 4, tpb, d, ahead=1, last_tile=last),
            _mod_spec(layer, 5, tpb, d),
            pl.BlockSpec((None, d, tf), lambda i, f: (layer, 0, f)),
            pl.BlockSpec((None, tf, d), lambda i, f: (layer, f, 0)),
            pl.BlockSpec((1, d), lambda i, f: (0, 0)),
        ],
        out_specs=pl.BlockSpec((tm, d), lambda i, f: (i, 0)),
        out_shape=jax.ShapeDtypeStruct((m, d), F32),
        scratch_shapes=[pltpu.VMEM((2, tm, d), BF16)],
        compiler_params=pltpu.CompilerParams(dimension_semantics=("arbitrary", "arbitrary"),
                                             vmem_limit_bytes=MLP_VMEM_LIMIT_BYTES),
        name="mlp_final" if final_norm else "mlp",
    )(x, x, x, gain, mod, mod, mod, mod, mod, w_up, w_down, fin_gain)


def kernel(x, c, ada_w, ada_b, norm_mix, norm_mlp, fox_w_in, fox_b_f, fox_w_out, conv_w_in, conv_w,
           conv_w_out, mlp_w_up, mlp_w_down, final_norm):
    batch, seq, d = x.shape
    depth = ada_w.shape[0]
    dh = d // N_HEADS
    assert depth == 2 and dh == LANES and fox_w_in.shape[-1] == 3 * d + N_HEADS

    c_pad = jnp.pad(c, ((0, BF16_SUBLANES - batch), (0, 0)))
    mod = _ada_modulation(c_pad, ada_w, ada_b)
    mod = mod[:, :batch].reshape(depth, batch, N_MOD, 1, d)

    xf = x.reshape(batch * seq, d)
    row = lambda v: v.reshape(1, d)

    b_f = jnp.pad(fox_b_f[0], (0, LANES - N_HEADS)).reshape(1, LANES)
    w_fox_in = fox_w_in[0].astype(BF16)
    qkv, f_terms, w_attn_out, w_conv_in, w_conv_out = _inproj(
        xf, row(norm_mix[0]), mod, 0, w_fox_in, mixer="fox", side=w_fox_in, seq=seq,
        out_scale=LOG2_E * dh ** -0.5, casts=(fox_w_out[0], conv_w_in[0], conv_w_out[0]),
        forget_bias=b_f)
    attn, w_up, w_down = _fox_attention(qkv, f_terms, mlp_w_up, mlp_w_down, batch=batch, seq=seq,
                                        d=d)
    xf = _outproj(attn, w_attn_out, xf, mod, 0, seq=seq)
    xf = _mlp(xf, row(norm_mlp[0]), mod, 0, w_up, w_down, row(final_norm), seq=seq,
              final_norm=False)

    gated = _inproj(xf, row(norm_mix[1]), mod, 1, w_conv_in, mixer="conv", side=conv_w[0],
                    seq=seq)[0]
    xf = _outproj(gated, w_conv_out, xf, mod, 1, seq=seq)
    xf = _mlp(xf, row(norm_mlp[1]), mod, 1, w_up, w_down, row(final_norm), seq=seq,
              final_norm=True)
    return xf.reshape(batch, seq, d)
```

```python
import functools

import jax
import jax.numpy as jnp
from jax import lax
from jax.experimental import pallas as pl
from jax.experimental.pallas import tpu as pltpu

N_HEADS = 16
CONV_WIDTH = 3
N_MOD = 6
RMS_EPS = 1e-6
LOG2_E = 1.4426950408889634

LANES = 128
F32_SUBLANES = 8
BF16_SUBLANES = 16
V7X_VMEM_BYTES = 64 * 1024 * 1024
VMEM_LIMIT_BYTES = V7X_VMEM_BYTES - 8 * 1024 * 1024
MLP_VMEM_LIMIT_BYTES = V7X_VMEM_BYTES - 4 * 1024 * 1024
MLP_CHUNKS = 2
ADA_SIDE_COLS = 256

F32 = jnp.float32
BF16 = jnp.bfloat16


def _params(*semantics):
    return pltpu.CompilerParams(dimension_semantics=semantics, vmem_limit_bytes=VMEM_LIMIT_BYTES)


def _dot(a, b):
    return jnp.dot(a, b, preferred_element_type=F32)


def _norm_modulate(x, gain, shift, scale):
    inv = lax.rsqrt(jnp.mean(x * x, axis=-1, keepdims=True) + RMS_EPS)
    return (x * inv) * gain * (1.0 + scale) + shift


def _ada_kernel(c_ref, w_ref, b_ref, o_ref):
    c = c_ref[...]
    act = (c * jax.nn.sigmoid(c)).astype(BF16)
    o_ref[...] = _dot(act, w_ref[...].astype(BF16)) + b_ref[...]


def _ada_modulation(c_pad, ada_w, ada_b3, layers, tn=1024):
    _, d, n = ada_w.shape
    rows = c_pad.shape[0]
    return pl.pallas_call(
        _ada_kernel,
        grid=(layers, n // tn),
        in_specs=[
            pl.BlockSpec((rows, d), lambda l, j: (0, 0)),
            pl.BlockSpec((None, d, tn), lambda l, j: (l, 0, j)),
            pl.BlockSpec((None, 1, tn), lambda l, j: (l, 0, j)),
        ],
        out_specs=pl.BlockSpec((None, rows, tn), lambda l, j: (l, 0, j)),
        out_shape=jax.ShapeDtypeStruct((layers, rows, n), F32),
        compiler_params=_params("parallel", "parallel"),
        name="ada_modulation",
    )(c_pad, ada_w, ada_b3)


def _mod_spec(layer, which, tiles_per_batch, d, ahead=0, last_tile=None):
    def index(i, *_):
        tile = jnp.minimum(i + ahead, last_tile) if ahead else i
        return (layer, tile // tiles_per_batch, which, 0, 0)
    return pl.BlockSpec((None, None, None, 1, d), index)


def _inproj_kernel(*refs, mixer, out_scale, n_casts, tiles_per_seq):
    x0_ref, xn_ref, g_ref, sh0_ref, sc0_ref, shn_ref, scn_ref, w_ref, side_ref = refs[:9]
    refs = refs[9:]
    if mixer == "fox":
        bf_ref, refs = refs[0], refs[1:]
    cast_in, refs = refs[:n_casts], refs[n_casts:]
    o_ref, refs = refs[0], refs[1:]
    if mixer == "fox":
        f_ref, refs = refs[0], refs[1:]
    cast_out, scratch = refs[:n_casts], refs[n_casts:]
    h_scr = scratch[0]
    for src, dst in zip(cast_in, cast_out):
        dst[...] = src[...].astype(dst.dtype)
    i = pl.program_id(0)
    slot = i % 2
    tm, d = xn_ref.shape

    @pl.when(i == 0)
    def _():
        h_scr[0] = _norm_modulate(x0_ref[...], g_ref[...], sh0_ref[...], sc0_ref[...]).astype(BF16)
        scratch[2][...] = jnp.zeros_like(scratch[2])
        if mixer == "fox":
            row = lax.broadcasted_iota(jnp.int32, (tm, tm), 0)
            col = lax.broadcasted_iota(jnp.int32, (tm, tm), 1)
            scratch[1][...] = jnp.where(col <= row, 1.0, 0.0).astype(BF16)

    def column_group(grp):
        return _dot(h_scr[slot], w_ref[:, grp * d:(grp + 1) * d])

    def normalise_next(part, parts):
        rows = slice(part * (tm // parts), (part + 1) * (tm // parts))
        h_scr[1 - slot, rows, :] = _norm_modulate(xn_ref[rows, :], g_ref[...], shn_ref[...],
                                                  scn_ref[...]).astype(BF16)

    if mixer == "fox":
        tri, carry = scratch[1], scratch[2]
        lane = lax.broadcasted_iota(jnp.int32, (1, LANES), 1)
        z = _dot(h_scr[slot], jnp.where(lane < N_HEADS, side_ref[...], 0.0).astype(BF16))
        z = z + bf_ref[...]
        log_f = jnp.minimum(z, 0.0) - jnp.log1p(jnp.exp(-jnp.abs(z)))
        o_ref[:, 0:d] = (column_group(0) * out_scale).astype(o_ref.dtype)
        normalise_next(0, 2)
        sums = _dot(tri[...], jnp.concatenate(_split3(log_f), axis=1).astype(BF16))
        o_ref[:, d:2 * d] = column_group(1).astype(o_ref.dtype)
        normalise_next(1, 2)
        csum = (sums[:, 0:LANES] + sums[:, LANES:2 * LANES]) + sums[:, 2 * LANES:]
        csum = csum + jnp.where(i % tiles_per_seq == 0, 0.0, carry[...])
        carry[...] = csum[tm - 1:tm, :]
        neg_f = (csum * -LOG2_E).T[0:N_HEADS, :]
        for plane, term in enumerate(_split3(neg_f)):
            f_ref[plane] = term
        o_ref[:, 2 * d:3 * d] = column_group(2).astype(o_ref.dtype)
    else:
        ext, carry = scratch[1], scratch[2]
        c_gate = column_group(1)
        normalise_next(0, 2)
        gated = c_gate * column_group(2)
        normalise_next(1, 2)
        halo = carry.shape[0]
        ext[0:halo, :] = jnp.where(i % tiles_per_seq == 0, 0.0, carry[...])
        ext[halo:halo + tm, :] = gated
        carry[...] = gated[tm - halo:tm, :]
        y = None
        for tap in range(CONV_WIDTH):
            start = halo - (CONV_WIDTH - 1 - tap)
            term = side_ref[tap:tap + 1, :] * ext[start:start + tm, :]
            y = term if y is None else y + term
        o_ref[...] = (column_group(0) * y).astype(o_ref.dtype)


def _inproj(x, gain, mod, layer, w, *, mixer, seq, side, out_scale=1.0, tm=256, casts=(),
            forget_bias=None):
    m, d = x.shape
    n = 3 * d
    tpb = seq // tm
    last = m // tm - 1
    steps = m // tm
    slab = lambda a: pl.BlockSpec((a.shape[0] // steps, a.shape[1]), lambda i: (i, 0))
    in_specs = [
        pl.BlockSpec((tm, d), lambda i: (0, 0)),
        pl.BlockSpec((tm, d), lambda i: (jnp.minimum(i + 1, last), 0)),
        pl.BlockSpec((1, d), lambda i: (0, 0)),
        _mod_spec(layer, 0, tpb, d),
        _mod_spec(layer, 1, tpb, d),
        _mod_spec(layer, 0, tpb, d, ahead=1, last_tile=last),
        _mod_spec(layer, 1, tpb, d, ahead=1, last_tile=last),
        pl.BlockSpec((d, n), lambda i: (0, 0), pipeline_mode=pl.Buffered(1)),
    ]
    args = [x, x, gain, mod, mod, mod, mod, w, side]
    scratch = [pltpu.VMEM((2, tm, d), BF16)]
    if mixer == "fox":
        in_specs += [pl.BlockSpec((d, LANES), lambda i: (0, n // LANES)),
                     pl.BlockSpec((1, LANES), lambda i: (0, 0))]
        args.append(forget_bias)
        out_specs = [pl.BlockSpec((tm, n), lambda i: (i, 0)),
                     pl.BlockSpec((None, 3, N_HEADS, tm), lambda i: (i // tpb, 0, 0, i % tpb))]
        out_shape = [jax.ShapeDtypeStruct((m, n), BF16),
                     jax.ShapeDtypeStruct((m // seq, 3, N_HEADS, seq), F32)]
        scratch += [pltpu.VMEM((tm, tm), BF16), pltpu.VMEM((1, LANES), F32)]
    else:
        in_specs.append(pl.BlockSpec(side.shape, lambda i: (0, 0)))
        out_specs = [pl.BlockSpec((tm, d), lambda i: (i, 0))]
        out_shape = [jax.ShapeDtypeStruct((m, d), BF16)]
        scratch += [pltpu.VMEM((F32_SUBLANES + tm, d), F32), pltpu.VMEM((F32_SUBLANES, d), F32)]
    for a in casts:
        assert (a.shape[0] // steps) % BF16_SUBLANES == 0
        in_specs.append(slab(a))
        args.append(a)
        out_specs.append(slab(a))
        out_shape.append(jax.ShapeDtypeStruct(a.shape, BF16))
    kern = functools.partial(_inproj_kernel, mixer=mixer, out_scale=out_scale, n_casts=len(casts),
                             tiles_per_seq=tpb)
    return pl.pallas_call(
        kern,
        grid=(m // tm,),
        in_specs=in_specs,
        out_specs=out_specs,
        out_shape=out_shape,
        scratch_shapes=scratch,
        compiler_params=_params("arbitrary"),
        name="inproj_" + mixer,
    )(*args)


def _split3(x):
    hi = x.astype(BF16).astype(F32)
    r = x - hi
    mid = r.astype(BF16).astype(F32)
    lo = (r - mid).astype(BF16).astype(F32)
    return hi, mid, lo


def _fox_attn_kernel(q_ref, k_ref, v_ref, f_ref, wu_ref, wd_ref, c_ref, aw_ref, ab_ref,
                     o_ref, wu_bf_ref, wd_bf_ref, ada_ref, kaug, vt, qt, *, tq, tk):
    wu_bf_ref[...] = wu_ref[...].astype(BF16)
    wd_bf_ref[...] = wd_ref[...].astype(BF16)
    _ada_kernel(c_ref, aw_ref, ab_ref, ada_ref)
    seq, dh = k_ref.shape
    head = pl.ds(pl.program_id(1), 1)
    sub = lax.broadcasted_iota(jnp.int32, (F32_SUBLANES, 1), 0)

    for c in range(seq // tk):
        rows = slice(c * tk, (c + 1) * tk)
        f_rows = jnp.where(sub == 0, f_ref[0, head, rows], jnp.where(
            sub == 1, f_ref[1, head, rows], jnp.where(sub == 2, f_ref[2, head, rows], 0.0)))
        f_cols = jnp.concatenate([f_rows, jnp.zeros((LANES - F32_SUBLANES, tk), F32)],
                                 axis=0).T
        kaug[rows, 0:dh] = k_ref[rows, :]
        kaug[rows, dh:dh + LANES] = f_cols.astype(BF16)
        vt[0:dh, rows] = v_ref[rows, :].T
        qt[0:dh, rows] = q_ref[rows, :].T
    pad_row = lax.broadcasted_iota(jnp.int32, (BF16_SUBLANES, 1), 0)
    vt[dh:dh + BF16_SUBLANES, :] = jnp.broadcast_to(
        jnp.where(pad_row == 0, 1.0, 0.0).astype(BF16), (BF16_SUBLANES, seq))
    aug_row = lax.broadcasted_iota(jnp.int32, (LANES, 1), 0)
    qt[dh:dh + LANES, :] = jnp.broadcast_to(
        jnp.where(aug_row < 3, 1.0, 0.0).astype(BF16), (LANES, seq))

    key_in_chunk = lax.broadcasted_iota(jnp.int32, (tk, tq), 0)
    query_in_block = lax.broadcasted_iota(jnp.int32, (tk, tq), 1)
    blocks = [(qi, j) for qi in range(seq // tq) for j in range(pl.cdiv((qi + 1) * tq, tk))]

    def logits(qi, j):
        return _dot(kaug[j * tk:(j + 1) * tk, :], qt[:, qi * tq:(qi + 1) * tq])

    s_next = logits(*blocks[0])
    for t, (qi, j) in enumerate(blocks):
        s = s_next
        if t + 1 < len(blocks):
            s_next = logits(*blocks[t + 1])
        q_lo, k_lo = qi * tq, j * tk
        if j == 0:
            m = jnp.full((1, tq), -jnp.inf, F32)
            acc = jnp.zeros((dh + BF16_SUBLANES, tq), F32)
        if k_lo + tk - 1 > q_lo:
            s = jnp.where(key_in_chunk + k_lo <= query_in_block + q_lo, s, -jnp.inf)
        m_new = jnp.maximum(m, jnp.max(s, axis=0, keepdims=True))
        alpha = jnp.exp2(m - m_new)
        p = jnp.exp2(s - m_new)
        acc = alpha * acc + _dot(vt[:, k_lo:k_lo + tk], p.astype(BF16))
        m = m_new
        if k_lo + tk >= q_lo + tq:
            inv_l = 1.0 / acc[dh:dh + 1, :]
            o_ref[q_lo:q_lo + tq, :] = (acc[0:dh, :] * inv_l).T.astype(o_ref.dtype)


def _fox_attention(qkv, f_terms, w_up, w_down, ada, *, batch, seq, d, tq=512, tk=512):
    dh = d // N_HEADS
    qkv3 = qkv.reshape(batch, seq, 3 * d)
    head_cols = lambda part: pl.BlockSpec((None, seq, dh), lambda b, h: (b, 0, part * N_HEADS + h))
    steps = batch * N_HEADS
    wu2 = w_up.reshape(-1, w_up.shape[-1])
    wd2 = w_down.reshape(-1, w_down.shape[-1])
    slab = lambda w2: pl.BlockSpec((w2.shape[0] // steps, w2.shape[1]),
                                   lambda b, h: (b * N_HEADS + h, 0))
    c_pad, ada_w, ada_b3, first_layer = ada
    depth, _, n_mod = ada_w.shape
    per_layer = n_mod // ADA_SIDE_COLS
    n_blocks = (depth - first_layer) * per_layer
    assert n_blocks <= steps
    block = lambda b, h: jnp.minimum(b * N_HEADS + h, n_blocks - 1)
    ada_in = lambda rows: pl.BlockSpec(
        (None, rows, ADA_SIDE_COLS),
        lambda b, h: (first_layer + block(b, h) // per_layer, 0, block(b, h) % per_layer))
    attn, wu_bf, wd_bf, ada_rest = pl.pallas_call(
        functools.partial(_fox_attn_kernel, tq=tq, tk=tk),
        grid=(batch, N_HEADS),
        in_specs=[
            head_cols(0),
            head_cols(1),
            head_cols(2),
            pl.BlockSpec((None, 3, N_HEADS, seq), lambda b, h: (b, 0, 0, 0)),
            slab(wu2),
            slab(wd2),
            pl.BlockSpec(c_pad.shape, lambda b, h: (0, 0)),
            ada_in(d),
            ada_in(1),
        ],
        out_specs=[head_cols(0), slab(wu2), slab(wd2),
                   pl.BlockSpec((c_pad.shape[0], ADA_SIDE_COLS), lambda b, h: (0, block(b, h)))],
        out_shape=[jax.ShapeDtypeStruct((batch, seq, d), BF16),
                   jax.ShapeDtypeStruct(wu2.shape, BF16),
                   jax.ShapeDtypeStruct(wd2.shape, BF16),
                   jax.ShapeDtypeStruct((c_pad.shape[0], n_blocks * ADA_SIDE_COLS), F32)],
        scratch_shapes=[pltpu.VMEM((seq, dh + LANES), BF16),
                        pltpu.VMEM((dh + BF16_SUBLANES, seq), BF16),
                        pltpu.VMEM((dh + LANES, seq), BF16)],
        compiler_params=_params("arbitrary", "arbitrary"),
        name="fox_attention",
    )(qkv3, qkv3, qkv3, f_terms, wu2, wd2, c_pad, ada_w, ada_b3)
    return (attn.reshape(batch * seq, d), wu_bf.reshape(w_up.shape), wd_bf.reshape(w_down.shape),
            ada_rest.reshape(c_pad.shape[0], depth - first_layer, n_mod))


def _outproj_kernel(a_ref, w_ref, x_ref, g_ref, o_ref):
    o_ref[...] = x_ref[...] + g_ref[...] * _dot(a_ref[...], w_ref[...])


def _outproj(a, w, x, mod, layer, *, seq, tm=512):
    m, d = x.shape
    tpb = seq // tm
    return pl.pallas_call(
        _outproj_kernel,
        grid=(m // tm,),
        in_specs=[
            pl.BlockSpec((tm, d), lambda i: (i, 0)),
            pl.BlockSpec((d, d), lambda i: (0, 0)),
            pl.BlockSpec((tm, d), lambda i: (i, 0)),
            _mod_spec(layer, 2, tpb, d),
        ],
        out_specs=pl.BlockSpec((tm, d), lambda i: (i, 0)),
        out_shape=jax.ShapeDtypeStruct((m, d), F32),
        compiler_params=_params("parallel"),
        name="mixer_outproj",
    )(a, w, x, mod)


def _mlp_kernel(x0_ref, xs_ref, xn_ref, g_ref, sh_ref, sc_ref, shn_ref, scn_ref, gate_ref, wu_ref,
                wd_ref, fin_ref, o_ref, h_scr, *, final_norm):
    i = pl.program_id(0)
    f = pl.program_id(1)
    slot = i % 2
    rows_per_step = xs_ref.shape[0]
    chunk = wu_ref.shape[1] // MLP_CHUNKS
    half = chunk // 2

    @pl.when(jnp.logical_and(i == 0, f == 0))
    def _():
        h_scr[0] = _norm_modulate(x0_ref[...], g_ref[...], sh_ref[...], sc_ref[...]).astype(BF16)

    row0 = pl.multiple_of(f * rows_per_step, rows_per_step)

    def hidden_block(first):
        if not first:
            o_ref[pl.ds(row0, rows_per_step), :] += xs_ref[...]
        for c in range(MLP_CHUNKS):
            lo = c * chunk
            u_lo = jnp.maximum(_dot(h_scr[slot], wu_ref[:, lo:lo + half]), 0.0)
            if c == 0:
                h_scr[1 - slot, pl.ds(row0, rows_per_step), :] = _norm_modulate(
                    xn_ref[...], g_ref[...], shn_ref[...], scn_ref[...]).astype(BF16)
            u_hi = jnp.maximum(_dot(h_scr[slot], wu_ref[:, lo + half:lo + chunk]), 0.0)
            u = jnp.concatenate([u_lo, u_hi], axis=1)
            part = gate_ref[...] * _dot((u * u).astype(BF16), wd_ref[lo:lo + chunk, :])
            if first and c == 0:
                o_ref[...] = part
            else:
                o_ref[...] += part
        if first:
            o_ref[pl.ds(row0, rows_per_step), :] += xs_ref[...]

    pl.when(f == 0)(functools.partial(hidden_block, True))
    pl.when(f > 0)(functools.partial(hidden_block, False))

    if final_norm:
        @pl.when(f == pl.num_programs(1) - 1)
        def _():
            y = o_ref[...]
            inv = lax.rsqrt(jnp.mean(y * y, axis=-1, keepdims=True) + RMS_EPS)
            o_ref[...] = (y * inv) * fin_ref[...]


def _mlp(x, gain, mod, layer, w_up, w_down, fin_gain, *, seq, final_norm, tm=512, tf=2048):
    m, d = x.shape
    dff = w_up.shape[2]
    nf = dff // tf
    tpb = seq // tm
    last = m // tm - 1
    rows_per_step = tm // nf
    assert (rows_per_step // MLP_CHUNKS) % BF16_SUBLANES == 0
    return pl.pallas_call(
        functools.partial(_mlp_kernel, final_norm=final_norm),
        grid=(m // tm, nf),
        in_specs=[
            pl.BlockSpec((tm, d), lambda i, f: (0, 0), pipeline_mode=pl.Buffered(1)),
            pl.BlockSpec((rows_per_step, d), lambda i, f: (i * nf + f, 0)),
            pl.BlockSpec((rows_per_step, d), lambda i, f: (jnp.minimum(i + 1, last) * nf + f, 0)),
            pl.BlockSpec((1, d), lambda i, f: (0, 0)),
            _mod_spec(layer, 3, tpb, d),
            _mod_spec(layer, 4, tpb, d),
            _mod_spec(layer, 3, tpb, d, ahead=1, last_tile=last),
            _mod_spec(layer, 4, tpb, d, ahead=1, last_tile=last),
            _mod_spec(layer, 5, tpb, d),
            pl.BlockSpec((None, d, tf), lambda i, f: (layer, 0, f)),
            pl.BlockSpec((None, tf, d), lambda i, f: (layer, f, 0)),
            pl.BlockSpec((1, d), lambda i, f: (0, 0)),
        ],
        out_specs=pl.BlockSpec((tm, d), lambda i, f: (i, 0)),
        out_shape=jax.ShapeDtypeStruct((m, d), F32),
        scratch_shapes=[pltpu.VMEM((2, tm, d), BF16)],
        compiler_params=pltpu.CompilerParams(dimension_semantics=("arbitrary", "arbitrary"),
                                             vmem_limit_bytes=MLP_VMEM_LIMIT_BYTES),
        name="mlp_final" if final_norm else "mlp",
    )(x, x, x, gain, mod, mod, mod, mod, mod, w_up, w_down, fin_gain)


def kernel(x, c, ada_w, ada_b, norm_mix, norm_mlp, fox_w_in, fox_b_f, fox_w_out, conv_w_in, conv_w,
           conv_w_out, mlp_w_up, mlp_w_down, final_norm):
    batch, seq, d = x.shape
    depth = ada_w.shape[0]
    dh = d // N_HEADS
    assert depth == 2 and dh == LANES and fox_w_in.shape[-1] == 3 * d + N_HEADS

    c_pad = jnp.pad(c, ((0, BF16_SUBLANES - batch), (0, 0)))
    ada_b3 = ada_b.reshape(depth, 1, N_MOD * d)
    table = lambda t: t[:, :batch].reshape(t.shape[0], batch, N_MOD, 1, d)
    mod0 = _ada_modulation(c_pad, ada_w, ada_b3, layers=1)
    mod = table(mod0)

    xf = x.reshape(batch * seq, d)
    row = lambda v: v.reshape(1, d)

    b_f = jnp.pad(fox_b_f[0], (0, LANES - N_HEADS)).reshape(1, LANES)
    w_fox_in = fox_w_in[0].astype(BF16)
    qkv, f_terms, w_attn_out, w_conv_in, w_conv_out = _inproj(
        xf, row(norm_mix[0]), mod, 0, w_fox_in, mixer="fox", side=w_fox_in, seq=seq,
        out_scale=LOG2_E * dh ** -0.5, casts=(fox_w_out[0], conv_w_in[0], conv_w_out[0]),
        forget_bias=b_f)
    attn, w_up, w_down, mod_rest = _fox_attention(
        qkv, f_terms, mlp_w_up, mlp_w_down, (c_pad, ada_w, ada_b3, 1), batch=batch, seq=seq, d=d)
    mod = table(jnp.concatenate([mod0, jnp.swapaxes(mod_rest, 0, 1)], axis=0))
    xf = _outproj(attn, w_attn_out, xf, mod, 0, seq=seq)
    xf = _mlp(xf, row(norm_mlp[0]), mod, 0, w_up, w_down, row(final_norm), seq=seq,
              final_norm=False)

    gated = _inproj(xf, row(norm_mix[1]), mod, 1, w_conv_in, mixer="conv", side=conv_w[0],
                    seq=seq)[0]
    xf = _outproj(gated, w_conv_out, xf, mod, 1, seq=seq)
    xf = _mlp(xf, row(norm_mlp[1]), mod, 1, w_up, w_down, row(final_norm), seq=seq,
              final_norm=True)
    return xf.reshape(batch, seq, d)
```

```python
import functools

import jax
import jax.numpy as jnp
from jax import lax
from jax.experimental import pallas as pl
from jax.experimental.pallas import tpu as pltpu

N_HEADS = 16
CONV_WIDTH = 3
N_MOD = 6
RMS_EPS = 1e-6
LOG2_E = 1.4426950408889634

LANES = 128
F32_SUBLANES = 8
BF16_SUBLANES = 16
V7X_VMEM_BYTES = 64 * 1024 * 1024
VMEM_LIMIT_BYTES = V7X_VMEM_BYTES - 8 * 1024 * 1024
MLP_VMEM_LIMIT_BYTES = V7X_VMEM_BYTES - 4 * 1024 * 1024
MLP_CHUNKS = 2
ADA_SIDE_COLS = 256

F32 = jnp.float32
BF16 = jnp.bfloat16


def _params(*semantics):
    return pltpu.CompilerParams(dimension_semantics=semantics, vmem_limit_bytes=VMEM_LIMIT_BYTES)


def _dot(a, b):
    return jnp.dot(a, b, preferred_element_type=F32)


def _norm_modulate(x, gain, shift, scale):
    inv = lax.rsqrt(jnp.mean(x * x, axis=-1, keepdims=True) + RMS_EPS)
    return (x * inv) * gain * (1.0 + scale) + shift


def _ada_kernel(c_ref, w_ref, b_ref, o_ref):
    c = c_ref[...]
    act = (c * jax.nn.sigmoid(c)).astype(BF16)
    o_ref[...] = _dot(act, w_ref[...].astype(BF16)) + b_ref[...]


def _ada_modulation(c_pad, ada_w, ada_b3, layers, tn=1024):
    _, d, n = ada_w.shape
    rows = c_pad.shape[0]
    return pl.pallas_call(
        _ada_kernel,
        grid=(layers, n // tn),
        in_specs=[
            pl.BlockSpec((rows, d), lambda l, j: (0, 0)),
            pl.BlockSpec((None, d, tn), lambda l, j: (l, 0, j)),
            pl.BlockSpec((None, 1, tn), lambda l, j: (l, 0, j)),
        ],
        out_specs=pl.BlockSpec((None, rows, tn), lambda l, j: (l, 0, j)),
        out_shape=jax.ShapeDtypeStruct((layers, rows, n), F32),
        compiler_params=_params("parallel", "parallel"),
        name="ada_modulation",
    )(c_pad, ada_w, ada_b3)


def _mod_spec(layer, which, tiles_per_batch, d, ahead=0, last_tile=None):
    def index(i, *_):
        tile = jnp.minimum(i + ahead, last_tile) if ahead else i
        return (layer, tile // tiles_per_batch, which, 0, 0)
    return pl.BlockSpec((None, None, None, 1, d), index)


def _inproj_kernel(*refs, mixer, out_scale, n_casts, tiles_per_seq):
    x0_ref, xn_ref, g_ref, sh0_ref, sc0_ref, shn_ref, scn_ref, w_ref, side_ref = refs[:9]
    refs = refs[9:]
    if mixer == "fox":
        bf_ref, refs = refs[0], refs[1:]
    else:
        (xc_ref, gate_ref, wo_ref), refs = refs[:3], refs[3:]
    cast_in, refs = refs[:n_casts], refs[n_casts:]
    o_ref, refs = refs[0], refs[1:]
    if mixer == "fox":
        f_ref, refs = refs[0], refs[1:]
    cast_out, scratch = refs[:n_casts], refs[n_casts:]
    h_scr = scratch[0]
    for src, dst in zip(cast_in, cast_out):
        dst[...] = src[...].astype(dst.dtype)
    i = pl.program_id(0)
    slot = i % 2
    tm, d = xn_ref.shape

    @pl.when(i == 0)
    def _():
        h_scr[0] = _norm_modulate(x0_ref[...], g_ref[...], sh0_ref[...], sc0_ref[...]).astype(BF16)
        scratch[2][...] = jnp.zeros_like(scratch[2])
        if mixer == "fox":
            row = lax.broadcasted_iota(jnp.int32, (tm, tm), 0)
            col = lax.broadcasted_iota(jnp.int32, (tm, tm), 1)
            scratch[1][...] = jnp.where(col <= row, 1.0, 0.0).astype(BF16)

    def column_group(grp):
        return _dot(h_scr[slot], w_ref[:, grp * d:(grp + 1) * d])

    def normalise_next(part, parts):
        rows = slice(part * (tm // parts), (part + 1) * (tm // parts))
        h_scr[1 - slot, rows, :] = _norm_modulate(xn_ref[rows, :], g_ref[...], shn_ref[...],
                                                  scn_ref[...]).astype(BF16)

    if mixer == "fox":
        tri, carry = scratch[1], scratch[2]
        lane = lax.broadcasted_iota(jnp.int32, (1, LANES), 1)
        z = _dot(h_scr[slot], jnp.where(lane < N_HEADS, side_ref[...], 0.0).astype(BF16))
        z = z + bf_ref[...]
        log_f = jnp.minimum(z, 0.0) - jnp.log1p(jnp.exp(-jnp.abs(z)))
        o_ref[:, 0:d] = (column_group(0) * out_scale).astype(o_ref.dtype)
        normalise_next(0, 2)
        sums = _dot(tri[...], jnp.concatenate(_split3(log_f), axis=1).astype(BF16))
        o_ref[:, d:2 * d] = column_group(1).astype(o_ref.dtype)
        normalise_next(1, 2)
        csum = (sums[:, 0:LANES] + sums[:, LANES:2 * LANES]) + sums[:, 2 * LANES:]
        csum = csum + jnp.where(i % tiles_per_seq == 0, 0.0, carry[...])
        carry[...] = csum[tm - 1:tm, :]
        neg_f = (csum * -LOG2_E).T[0:N_HEADS, :]
        for plane, term in enumerate(_split3(neg_f)):
            f_ref[plane] = term
        o_ref[:, 2 * d:3 * d] = column_group(2).astype(o_ref.dtype)
    else:
        ext, carry = scratch[1], scratch[2]
        c_gate = column_group(1)
        normalise_next(0, 2)
        gated = c_gate * column_group(2)
        normalise_next(1, 2)
        halo = carry.shape[0]
        ext[0:halo, :] = jnp.where(i % tiles_per_seq == 0, 0.0, carry[...])
        ext[halo:halo + tm, :] = gated
        carry[...] = gated[tm - halo:tm, :]
        y = None
        for tap in range(CONV_WIDTH):
            start = halo - (CONV_WIDTH - 1 - tap)
            term = side_ref[tap:tap + 1, :] * ext[start:start + tm, :]
            y = term if y is None else y + term
        z = (column_group(0) * y).astype(BF16)
        o_ref[...] = xc_ref[...] + gate_ref[...] * _dot(z, wo_ref[...])


def _inproj(x, gain, mod, layer, w, *, mixer, seq, side, out_scale=1.0, tm=256, casts=(),
            forget_bias=None, w_out=None):
    m, d = x.shape
    n = 3 * d
    tpb = seq // tm
    last = m // tm - 1
    steps = m // tm
    slab = lambda a: pl.BlockSpec((a.shape[0] // steps, a.shape[1]), lambda i: (i, 0))
    in_specs = [
        pl.BlockSpec((tm, d), lambda i: (0, 0)),
        pl.BlockSpec((tm, d), lambda i: (jnp.minimum(i + 1, last), 0)),
        pl.BlockSpec((1, d), lambda i: (0, 0)),
        _mod_spec(layer, 0, tpb, d),
        _mod_spec(layer, 1, tpb, d),
        _mod_spec(layer, 0, tpb, d, ahead=1, last_tile=last),
        _mod_spec(layer, 1, tpb, d, ahead=1, last_tile=last),
        pl.BlockSpec((d, n), lambda i: (0, 0), pipeline_mode=pl.Buffered(1)),
    ]
    args = [x, x, gain, mod, mod, mod, mod, w, side]
    scratch = [pltpu.VMEM((2, tm, d), BF16)]
    if mixer == "fox":
        in_specs += [pl.BlockSpec((d, LANES), lambda i: (0, n // LANES)),
                     pl.BlockSpec((1, LANES), lambda i: (0, 0))]
        args.append(forget_bias)
        out_specs = [pl.BlockSpec((tm, n), lambda i: (i, 0)),
                     pl.BlockSpec((None, 3, N_HEADS, tm), lambda i: (i // tpb, 0, 0, i % tpb))]
        out_shape = [jax.ShapeDtypeStruct((m, n), BF16),
                     jax.ShapeDtypeStruct((m // seq, 3, N_HEADS, seq), F32)]
        scratch += [pltpu.VMEM((tm, tm), BF16), pltpu.VMEM((1, LANES), F32)]
    else:
        in_specs += [pl.BlockSpec(side.shape, lambda i: (0, 0)),
                     pl.BlockSpec((tm, d), lambda i: (i, 0)),
                     _mod_spec(layer, 2, tpb, d),
                     pl.BlockSpec((d, d), lambda i: (0, 0), pipeline_mode=pl.Buffered(1))]
        args += [x, mod, w_out]
        out_specs = [pl.BlockSpec((tm, d), lambda i: (i, 0))]
        out_shape = [jax.ShapeDtypeStruct((m, d), F32)]
        scratch += [pltpu.VMEM((F32_SUBLANES + tm, d), F32), pltpu.VMEM((F32_SUBLANES, d), F32)]
    for a in casts:
        assert (a.shape[0] // steps) % BF16_SUBLANES == 0
        in_specs.append(slab(a))
        args.append(a)
        out_specs.append(slab(a))
        out_shape.append(jax.ShapeDtypeStruct(a.shape, BF16))
    kern = functools.partial(_inproj_kernel, mixer=mixer, out_scale=out_scale, n_casts=len(casts),
                             tiles_per_seq=tpb)
    return pl.pallas_call(
        kern,
        grid=(m // tm,),
        in_specs=in_specs,
        out_specs=out_specs,
        out_shape=out_shape,
        scratch_shapes=scratch,
        compiler_params=_params("arbitrary"),
        name="inproj_" + mixer,
    )(*args)


def _split3(x):
    hi = x.astype(BF16).astype(F32)
    r = x - hi
    mid = r.astype(BF16).astype(F32)
    lo = (r - mid).astype(BF16).astype(F32)
    return hi, mid, lo


def _fox_attn_kernel(q_ref, k_ref, v_ref, f_ref, wu_ref, wd_ref, c_ref, aw_ref, ab_ref,
                     o_ref, wu_bf_ref, wd_bf_ref, ada_ref, kaug, vt, qt, *, tq, tk):
    wu_bf_ref[...] = wu_ref[...].astype(BF16)
    wd_bf_ref[...] = wd_ref[...].astype(BF16)
    _ada_kernel(c_ref, aw_ref, ab_ref, ada_ref)
    seq, dh = k_ref.shape
    head = pl.ds(pl.program_id(1), 1)
    sub = lax.broadcasted_iota(jnp.int32, (F32_SUBLANES, 1), 0)

    for c in range(seq // tk):
        rows = slice(c * tk, (c + 1) * tk)
        f_rows = jnp.where(sub == 0, f_ref[0, head, rows], jnp.where(
            sub == 1, f_ref[1, head, rows], jnp.where(sub == 2, f_ref[2, head, rows], 0.0)))
        f_cols = jnp.concatenate([f_rows, jnp.zeros((LANES - F32_SUBLANES, tk), F32)],
                                 axis=0).T
        kaug[rows, 0:dh] = k_ref[rows, :]
        kaug[rows, dh:dh + LANES] = f_cols.astype(BF16)
        vt[0:dh, rows] = v_ref[rows, :].T
        qt[0:dh, rows] = q_ref[rows, :].T
    pad_row = lax.broadcasted_iota(jnp.int32, (BF16_SUBLANES, 1), 0)
    vt[dh:dh + BF16_SUBLANES, :] = jnp.broadcast_to(
        jnp.where(pad_row == 0, 1.0, 0.0).astype(BF16), (BF16_SUBLANES, seq))
    aug_row = lax.broadcasted_iota(jnp.int32, (LANES, 1), 0)
    qt[dh:dh + LANES, :] = jnp.broadcast_to(
        jnp.where(aug_row < 3, 1.0, 0.0).astype(BF16), (LANES, seq))

    key_in_chunk = lax.broadcasted_iota(jnp.int32, (tk, tq), 0)
    query_in_block = lax.broadcasted_iota(jnp.int32, (tk, tq), 1)
    blocks = [(qi, j) for qi in range(seq // tq) for j in range(pl.cdiv((qi + 1) * tq, tk))]

    def logits(qi, j):
        return _dot(kaug[j * tk:(j + 1) * tk, :], qt[:, qi * tq:(qi + 1) * tq])

    s_next = logits(*blocks[0])
    for t, (qi, j) in enumerate(blocks):
        s = s_next
        if t + 1 < len(blocks):
            s_next = logits(*blocks[t + 1])
        q_lo, k_lo = qi * tq, j * tk
        if j == 0:
            m = jnp.full((1, tq), -jnp.inf, F32)
            acc = jnp.zeros((dh + BF16_SUBLANES, tq), F32)
        if k_lo + tk - 1 > q_lo:
            s = jnp.where(key_in_chunk + k_lo <= query_in_block + q_lo, s, -jnp.inf)
        m_new = jnp.maximum(m, jnp.max(s, axis=0, keepdims=True))
        alpha = jnp.exp2(m - m_new)
        p = jnp.exp2(s - m_new)
        acc = alpha * acc + _dot(vt[:, k_lo:k_lo + tk], p.astype(BF16))
        m = m_new
        if k_lo + tk >= q_lo + tq:
            inv_l = 1.0 / acc[dh:dh + 1, :]
            o_ref[q_lo:q_lo + tq, :] = (acc[0:dh, :] * inv_l).T.astype(o_ref.dtype)


def _fox_attention(qkv, f_terms, w_up, w_down, ada, *, batch, seq, d, tq=512, tk=512):
    dh = d // N_HEADS
    qkv3 = qkv.reshape(batch, seq, 3 * d)
    head_cols = lambda part: pl.BlockSpec((None, seq, dh), lambda b, h: (b, 0, part * N_HEADS + h))
    steps = batch * N_HEADS
    wu2 = w_up.reshape(-1, w_up.shape[-1])
    wd2 = w_down.reshape(-1, w_down.shape[-1])
    slab = lambda w2: pl.BlockSpec((w2.shape[0] // steps, w2.shape[1]),
                                   lambda b, h: (b * N_HEADS + h, 0))
    c_pad, ada_w, ada_b3, first_layer = ada
    depth, _, n_mod = ada_w.shape
    per_layer = n_mod // ADA_SIDE_COLS
    n_blocks = (depth - first_layer) * per_layer
    assert n_blocks <= steps
    block = lambda b, h: jnp.minimum(b * N_HEADS + h, n_blocks - 1)
    ada_in = lambda rows: pl.BlockSpec(
        (None, rows, ADA_SIDE_COLS),
        lambda b, h: (first_layer + block(b, h) // per_layer, 0, block(b, h) % per_layer))
    attn, wu_bf, wd_bf, ada_rest = pl.pallas_call(
        functools.partial(_fox_attn_kernel, tq=tq, tk=tk),
        grid=(batch, N_HEADS),
        in_specs=[
            head_cols(0),
            head_cols(1),
            head_cols(2),
            pl.BlockSpec((None, 3, N_HEADS, seq), lambda b, h: (b, 0, 0, 0)),
            slab(wu2),
            slab(wd2),
            pl.BlockSpec(c_pad.shape, lambda b, h: (0, 0)),
            ada_in(d),
            ada_in(1),
        ],
        out_specs=[head_cols(0), slab(wu2), slab(wd2),
                   pl.BlockSpec((c_pad.shape[0], ADA_SIDE_COLS), lambda b, h: (0, block(b, h)))],
        out_shape=[jax.ShapeDtypeStruct((batch, seq, d), BF16),
                   jax.ShapeDtypeStruct(wu2.shape, BF16),
                   jax.ShapeDtypeStruct(wd2.shape, BF16),
                   jax.ShapeDtypeStruct((c_pad.shape[0], n_blocks * ADA_SIDE_COLS), F32)],
        scratch_shapes=[pltpu.VMEM((seq, dh + LANES), BF16),
                        pltpu.VMEM((dh + BF16_SUBLANES, seq), BF16),
                        pltpu.VMEM((dh + LANES, seq), BF16)],
        compiler_params=_params("arbitrary", "arbitrary"),
        name="fox_attention",
    )(qkv3, qkv3, qkv3, f_terms, wu2, wd2, c_pad, ada_w, ada_b3)
    return (attn.reshape(batch * seq, d), wu_bf.reshape(w_up.shape), wd_bf.reshape(w_down.shape),
            ada_rest.reshape(c_pad.shape[0], depth - first_layer, n_mod))


def _outproj_kernel(a_ref, w_ref, x_ref, g_ref, o_ref):
    o_ref[...] = x_ref[...] + g_ref[...] * _dot(a_ref[...], w_ref[...])


def _outproj(a, w, x, mod, layer, *, seq, tm=512):
    m, d = x.shape
    tpb = seq // tm
    return pl.pallas_call(
        _outproj_kernel,
        grid=(m // tm,),
        in_specs=[
            pl.BlockSpec((tm, d), lambda i: (i, 0)),
            pl.BlockSpec((d, d), lambda i: (0, 0)),
            pl.BlockSpec((tm, d), lambda i: (i, 0)),
            _mod_spec(layer, 2, tpb, d),
        ],
        out_specs=pl.BlockSpec((tm, d), lambda i: (i, 0)),
        out_shape=jax.ShapeDtypeStruct((m, d), F32),
        compiler_params=_params("parallel"),
        name="mixer_outproj",
    )(a, w, x, mod)


def _mlp_kernel(x0_ref, xs_ref, xn_ref, g_ref, sh_ref, sc_ref, shn_ref, scn_ref, gate_ref, wu_ref,
                wd_ref, fin_ref, o_ref, h_scr, *, final_norm):
    i = pl.program_id(0)
    f = pl.program_id(1)
    slot = i % 2
    rows_per_step = xs_ref.shape[0]
    chunk = wu_ref.shape[1] // MLP_CHUNKS
    half = chunk // 2

    @pl.when(jnp.logical_and(i == 0, f == 0))
    def _():
        h_scr[0] = _norm_modulate(x0_ref[...], g_ref[...], sh_ref[...], sc_ref[...]).astype(BF16)

    row0 = pl.multiple_of(f * rows_per_step, rows_per_step)

    def hidden_block(first):
        if not first:
            o_ref[pl.ds(row0, rows_per_step), :] += xs_ref[...]
        for c in range(MLP_CHUNKS):
            lo = c * chunk
            u_lo = jnp.maximum(_dot(h_scr[slot], wu_ref[:, lo:lo + half]), 0.0)
            if c == 0:
                h_scr[1 - slot, pl.ds(row0, rows_per_step), :] = _norm_modulate(
                    xn_ref[...], g_ref[...], shn_ref[...], scn_ref[...]).astype(BF16)
            u_hi = jnp.maximum(_dot(h_scr[slot], wu_ref[:, lo + half:lo + chunk]), 0.0)
            u = jnp.concatenate([u_lo, u_hi], axis=1)
            part = gate_ref[...] * _dot((u * u).astype(BF16), wd_ref[lo:lo + chunk, :])
            if first and c == 0:
                o_ref[...] = part
            else:
                o_ref[...] += part
        if first:
            o_ref[pl.ds(row0, rows_per_step), :] += xs_ref[...]

    pl.when(f == 0)(functools.partial(hidden_block, True))
    pl.when(f > 0)(functools.partial(hidden_block, False))

    if final_norm:
        @pl.when(f == pl.num_programs(1) - 1)
        def _():
            y = o_ref[...]
            inv = lax.rsqrt(jnp.mean(y * y, axis=-1, keepdims=True) + RMS_EPS)
            o_ref[...] = (y * inv) * fin_ref[...]


def _mlp(x, gain, mod, layer, w_up, w_down, fin_gain, *, seq, final_norm, tm=512, tf=2048):
    m, d = x.shape
    dff = w_up.shape[2]
    nf = dff // tf
    tpb = seq // tm
    last = m // tm - 1
    rows_per_step = tm // nf
    assert (rows_per_step // MLP_CHUNKS) % BF16_SUBLANES == 0
    return pl.pallas_call(
        functools.partial(_mlp_kernel, final_norm=final_norm),
        grid=(m // tm, nf),
        in_specs=[
            pl.BlockSpec((tm, d), lambda i, f: (0, 0), pipeline_mode=pl.Buffered(1)),
            pl.BlockSpec((rows_per_step, d), lambda i, f: (i * nf + f, 0)),
            pl.BlockSpec((rows_per_step, d), lambda i, f: (jnp.minimum(i + 1, last) * nf + f, 0)),
            pl.BlockSpec((1, d), lambda i, f: (0, 0)),
            _mod_spec(layer, 3, tpb, d),
            _mod_spec(layer, 4, tpb, d),
            _mod_spec(layer, 3, tpb, d, ahead=1, last_tile=last),
            _mod_spec(layer, 4, tpb, d, ahead=1, last_tile=last),
            _mod_spec(layer, 5, tpb, d),
            pl.BlockSpec((None, d, tf), lambda i, f: (layer, 0, f)),
            pl.BlockSpec((None, tf, d), lambda i, f: (layer, f, 0)),
            pl.BlockSpec((1, d), lambda i, f: (0, 0)),
        ],
        out_specs=pl.BlockSpec((tm, d), lambda i, f: (i, 0)),
        out_shape=jax.ShapeDtypeStruct((m, d), F32),
        scratch_shapes=[pltpu.VMEM((2, tm, d), BF16)],
        compiler_params=pltpu.CompilerParams(dimension_semantics=("arbitrary", "arbitrary"),
                                             vmem_limit_bytes=MLP_VMEM_LIMIT_BYTES),
        name="mlp_final" if final_norm else "mlp",
    )(x, x, x, gain, mod, mod, mod, mod, mod, w_up, w_down, fin_gain)


def kernel(x, c, ada_w, ada_b, norm_mix, norm_mlp, fox_w_in, fox_b_f, fox_w_out, conv_w_in, conv_w,
           conv_w_out, mlp_w_up, mlp_w_down, final_norm):
    batch, seq, d = x.shape
    depth = ada_w.shape[0]
    dh = d // N_HEADS
    assert depth == 2 and dh == LANES and fox_w_in.shape[-1] == 3 * d + N_HEADS

    c_pad = jnp.pad(c, ((0, BF16_SUBLANES - batch), (0, 0)))
    ada_b3 = ada_b.reshape(depth, 1, N_MOD * d)
    table = lambda t: t[:, :batch].reshape(t.shape[0], batch, N_MOD, 1, d)
    mod0 = _ada_modulation(c_pad, ada_w, ada_b3, layers=1)
    mod = table(mod0)

    xf = x.reshape(batch * seq, d)
    row = lambda v: v.reshape(1, d)

    b_f = jnp.pad(fox_b_f[0], (0, LANES - N_HEADS)).reshape(1, LANES)
    w_fox_in = fox_w_in[0].astype(BF16)
    qkv, f_terms, w_attn_out, w_conv_in, w_conv_out = _inproj(
        xf, row(norm_mix[0]), mod, 0, w_fox_in, mixer="fox", side=w_fox_in, seq=seq,
        out_scale=LOG2_E * dh ** -0.5, casts=(fox_w_out[0], conv_w_in[0], conv_w_out[0]),
        forget_bias=b_f)
    attn, w_up, w_down, mod_rest = _fox_attention(
        qkv, f_terms, mlp_w_up, mlp_w_down, (c_pad, ada_w, ada_b3, 1), batch=batch, seq=seq, d=d)
    mod = table(jnp.concatenate([mod0, jnp.swapaxes(mod_rest, 0, 1)], axis=0))
    xf = _outproj(attn, w_attn_out, xf, mod, 0, seq=seq)
    xf = _mlp(xf, row(norm_mlp[0]), mod, 0, w_up, w_down, row(final_norm), seq=seq,
              final_norm=False)

    xf = _inproj(xf, row(norm_mix[1]), mod, 1, w_conv_in, mixer="conv", side=conv_w[0],
                 w_out=w_conv_out, seq=seq)[0]
    xf = _mlp(xf, row(norm_mlp[1]), mod, 1, w_up, w_down, row(final_norm), seq=seq,
              final_norm=True)
    return xf.reshape(batch, seq, d)
```
